```python
import jax, jax.numpy as jnp
from jax import lax
import numpy as np

D_MODEL = 1024
BATCH = 2
SEQ = 8192
DEPTH = 2

N_A_LAYERS = DEPTH // 2
N_B_LAYERS = DEPTH - N_A_LAYERS
HEAD_DIM = 64
SB_HEADS = D_MODEL // HEAD_DIM
SWA_Q_HEADS = D_MODEL // HEAD_DIM
SWA_KV_HEADS = 4
SWA_GROUP = SWA_Q_HEADS // SWA_KV_HEADS
WINDOW = 128
Q_BLOCK = 128
D_FF = 2816
ROPE_THETA = 10000.0
RMS_EPS = 1e-6
FFN_RES_SCALE = 0.5

kernel_name = "yoco_stickbreak_swa_sink_macaron"


def rms_norm(x, g):
    xf = x.astype(jnp.float32)
    y = xf * lax.rsqrt(jnp.mean(xf * xf, axis=-1, keepdims=True) + RMS_EPS)
    return (y * g.astype(jnp.float32)).astype(x.dtype)


def swiglu(x, w_in, w_out):
    gate, up = jnp.split(x @ w_in, 2, axis=-1)
    return (jax.nn.silu(gate) * up) @ w_out


def rotary(x, pos):
    half = HEAD_DIM // 2
    inv_freq = ROPE_THETA ** (-jnp.arange(half, dtype=jnp.float32) / half)
    ang = pos.astype(jnp.float32)[:, None] * inv_freq[None, :]
    cos = jnp.cos(ang)[None, :, None, :]
    sin = jnp.sin(ang)[None, :, None, :]
    xf = x.astype(jnp.float32)
    x1, x2 = xf[..., :half], xf[..., half:]
    return jnp.concatenate([x1 * cos - x2 * sin, x2 * cos + x1 * sin], axis=-1).astype(x.dtype)


def stick_breaking_attention(q, k, v):
    B, S, H, Dh = q.shape
    nb = S // Q_BLOCK
    scale = Dh ** -0.5
    key_pos = jnp.arange(S)
    qb = q.reshape(B, nb, Q_BLOCK, H, Dh).transpose(1, 0, 2, 3, 4)

    def block(args):
        qi, i = args
        z = jnp.einsum('bqhd,bshd->bhqs', qi, k).astype(jnp.float32) * scale
        q_pos = i * Q_BLOCK + jnp.arange(Q_BLOCK)
        strict = key_pos[None, :] < q_pos[:, None]
        log_beta = jax.nn.log_sigmoid(z)
        log_1m_beta = jnp.where(strict, log_beta - z, 0.0)
        suffix = lax.cumsum(log_1m_beta, axis=3, reverse=True) - log_1m_beta
        w = jnp.where(strict, jnp.exp(log_beta + suffix), 0.0)
        return jnp.einsum('bhqs,bshd->bqhd', w.astype(v.dtype), v)

    out = lax.map(block, (qb, jnp.arange(nb)))
    return out.transpose(1, 0, 2, 3, 4).reshape(B, S, H, Dh)


def sliding_window_sink_attention(q, k, v, sinks):
    B, S, Hq, Dh = q.shape
    nb = S // WINDOW
    qb = q.reshape(B, nb, WINDOW, SWA_KV_HEADS, SWA_GROUP, Dh)

    def band(t):
        tb = t.reshape(B, nb, WINDOW, SWA_KV_HEADS, Dh)
        prev = jnp.pad(tb[:, :-1], ((0, 0), (1, 0), (0, 0), (0, 0), (0, 0)))
        return jnp.concatenate([prev, tb], axis=2)

    kb, vb = band(k), band(v)
    s = jnp.einsum('bnqhgd,bnkhd->bnhgqk', qb, kb).astype(jnp.float32) * (Dh ** -0.5)
    qi = jnp.arange(WINDOW)[:, None]
    ki = jnp.arange(2 * WINDOW)[None, :]
    diff = qi + WINDOW - ki
    in_window = (diff >= 0) & (diff < WINDOW)
    blk = jnp.arange(nb)[:, None, None]
    valid = in_window[None] & ((blk > 0) | (ki[None] >= WINDOW))
    s = jnp.where(valid[None, :, None, None], s, -jnp.inf)
    sink = jnp.broadcast_to(
        sinks.astype(jnp.float32).reshape(SWA_KV_HEADS, SWA_GROUP)[None, None, :, :, None, None],
        s.shape[:-1] + (1,))
    p = jax.nn.softmax(jnp.concatenate([s, sink], axis=-1), axis=-1)[..., :-1]
    out = jnp.einsum('bnhgqk,bnkhd->bnqhgd', p.astype(v.dtype), vb)
    return out.reshape(B, S, Hq, Dh)


def setup_inputs(seed: int = 0) -> dict:
    key = jax.random.key(seed)
    ks = jax.random.split(key, 20)
    f32 = jnp.float32

    def w(k, shape, fan_in):
        return jax.random.normal(k, shape, f32) * (fan_in ** -0.5)

    def gain(k, shape):
        return 1.0 + 0.02 * jax.random.normal(k, shape, f32)

    return {
        "x": jax.random.normal(ks[0], (BATCH, SEQ, D_MODEL), f32),
        "ffn1_norm": gain(ks[1], (DEPTH, D_MODEL)),
        "ffn1_w_in": w(ks[2], (DEPTH, D_MODEL, 2 * D_FF), D_MODEL),
        "ffn1_w_out": w(ks[3], (DEPTH, D_FF, D_MODEL), D_FF),
        "mix_norm": gain(ks[4], (DEPTH, D_MODEL)),
        "ffn2_norm": gain(ks[5], (DEPTH, D_MODEL)),
        "ffn2_w_in": w(ks[6], (DEPTH, D_MODEL, 2 * D_FF), D_MODEL),
        "ffn2_w_out": w(ks[7], (DEPTH, D_FF, D_MODEL), D_FF),
        "sb_w_qkv": w(ks[8], (N_A_LAYERS, D_MODEL, 3 * SB_HEADS * HEAD_DIM), D_MODEL),
        "sb_w_o": w(ks[9], (N_A_LAYERS, SB_HEADS * HEAD_DIM, D_MODEL), SB_HEADS * HEAD_DIM),
        "kv_norm": gain(ks[10], (D_MODEL,)),
        "kv_w": w(ks[11], (D_MODEL, 2 * SWA_KV_HEADS * HEAD_DIM), D_MODEL),
        "swa_w_q": w(ks[12], (N_B_LAYERS, D_MODEL, SWA_Q_HEADS * HEAD_DIM), D_MODEL),
        "swa_sinks": 0.5 * jax.random.normal(ks[13], (N_B_LAYERS, SWA_Q_HEADS), f32),
        "swa_w_o": w(ks[14], (N_B_LAYERS, SWA_Q_HEADS * HEAD_DIM, D_MODEL), SWA_Q_HEADS * HEAD_DIM),
        "final_norm": gain(ks[15], (D_MODEL,)),
    }


def reference(x, ffn1_norm, ffn1_w_in, ffn1_w_out, mix_norm, ffn2_norm, ffn2_w_in, ffn2_w_out,
              sb_w_qkv, sb_w_o, kv_norm, kv_w, swa_w_q, swa_sinks, swa_w_o, final_norm):
    B, S, D = x.shape
    pos = jnp.arange(S)
    h = x
    k_shared = None
    v_shared = None
    for layer in range(DEPTH):
        h = h + FFN_RES_SCALE * swiglu(rms_norm(h, ffn1_norm[layer]), ffn1_w_in[layer], ffn1_w_out[layer])
        hn = rms_norm(h, mix_norm[layer])
        if layer < N_A_LAYERS:
            qkv = (hn @ sb_w_qkv[layer]).reshape(B, S, 3, SB_HEADS, HEAD_DIM)
            o = stick_breaking_attention(qkv[:, :, 0], qkv[:, :, 1], qkv[:, :, 2])
            h = h + o.reshape(B, S, SB_HEADS * HEAD_DIM) @ sb_w_o[layer]
        else:
            j = layer - N_A_LAYERS
            q = rotary((hn @ swa_w_q[j]).reshape(B, S, SWA_Q_HEADS, HEAD_DIM), pos)
            o = sliding_window_sink_attention(q, k_shared, v_shared, swa_sinks[j])
            h = h + o.reshape(B, S, SWA_Q_HEADS * HEAD_DIM) @ swa_w_o[j]
        h = h + FFN_RES_SCALE * swiglu(rms_norm(h, ffn2_norm[layer]), ffn2_w_in[layer], ffn2_w_out[layer])
        if layer == N_A_LAYERS - 1:
            kv = (rms_norm(h, kv_norm) @ kv_w).reshape(B, S, 2, SWA_KV_HEADS, HEAD_DIM)
            k_shared = rotary(kv[:, :, 0], pos)
            v_shared = kv[:, :, 1]
    return rms_norm(h, final_norm)
```

```python
import functools

import jax
import jax.numpy as jnp
from jax import lax
from jax.experimental import pallas as pl
from jax.experimental.pallas import tpu as pltpu

F32 = jnp.float32
BF16 = jnp.bfloat16

D_MODEL = 1024
HEAD_DIM = 64
SB_HEADS = 16
SWA_Q_HEADS = 16
SWA_KV_HEADS = 4
SWA_GROUP = SWA_Q_HEADS // SWA_KV_HEADS
WINDOW = 128
D_FF = 2816
ROPE_THETA = 10000.0
RMS_EPS = 1e-6
FFN_RES_SCALE = 0.5
ATTN_SCALE = HEAD_DIM ** -0.5

LANES = 128
HEADS_PER_SLAB = LANES // HEAD_DIM
VMEM_LIMIT_BYTES = 56 * 1024 * 1024

TOKEN_TILE = 512
FF_CHUNKS = 2
SB_TILE = 256
F32_EXP_UNDERFLOW = -104.0


def _rms(x, g):
    return x * lax.rsqrt(jnp.mean(x * x, axis=-1, keepdims=True) + RMS_EPS) * g


def _params(semantics):
    return pltpu.CompilerParams(dimension_semantics=semantics, vmem_limit_bytes=VMEM_LIMIT_BYTES)


def _resident(shape):
    return pl.BlockSpec(shape, lambda *_: (0,) * len(shape), pipeline_mode=pl.Buffered(1))


def _ffn_kernel(*refs, has_attn, has_final_norm):
    refs = list(refs)
    h_ref = refs.pop(0)
    if has_attn:
        o_ref, wo_ref = refs.pop(0), refs.pop(0)
    g_ref, win_ref, wout_ref = refs.pop(0), refs.pop(0), refs.pop(0)
    if has_final_norm:
        fg_ref = refs.pop(0)
    (out_ref,) = refs

    x = h_ref[...]
    if has_attn:
        x = x + jnp.dot(o_ref[...], wo_ref[...], preferred_element_type=F32)
    xn = _rms(x, g_ref[...]).astype(BF16)
    fc = D_FF // FF_CHUNKS
    y = None
    for c in range(FF_CHUNKS):
        gate = jnp.dot(xn, win_ref[:, c * fc:(c + 1) * fc], preferred_element_type=F32)
        up = jnp.dot(xn, win_ref[:, D_FF + c * fc:D_FF + (c + 1) * fc], preferred_element_type=F32)
        act = (gate * jax.nn.sigmoid(gate) * up).astype(BF16)
        yc = jnp.dot(act, wout_ref[c * fc:(c + 1) * fc, :], preferred_element_type=F32)
        y = yc if y is None else y + yc
    x = x + FFN_RES_SCALE * y
    if has_final_norm:
        x = _rms(x, fg_ref[...])
    out_ref[...] = x


def _ffn(h, g, w_in, w_out, attn=None, final_g=None, name="ffn"):
    n, d = h.shape
    row = pl.BlockSpec((TOKEN_TILE, d), lambda i: (i, 0))
    args, specs = [h], [row]
    if attn is not None:
        o, w_o = attn
        args += [o, w_o]
        specs += [row, _resident(w_o.shape)]
    args += [g.reshape(1, d), w_in, w_out]
    specs += [_resident((1, d)), _resident(w_in.shape), _resident(w_out.shape)]
    if final_g is not None:
        args.append(final_g.reshape(1, d))
        specs.append(_resident((1, d)))
    return pl.pallas_call(
        functools.partial(_ffn_kernel, has_attn=attn is not None, has_final_norm=final_g is not None),
        out_shape=jax.ShapeDtypeStruct((n, d), F32),
        grid=(n // TOKEN_TILE,),
        in_specs=specs,
        out_specs=row,
        compiler_params=_params(("parallel",)),
        name=name,
    )(*args)


def _proj_kernel(h_ref, g_ref, w_ref, cos_ref, sin_ref, out_ref, *, rot_slabs):
    xn = _rms(h_ref[...], g_ref[...]).astype(BF16)
    y = jnp.dot(xn, w_ref[...], preferred_element_type=F32)
    if rot_slabs:
        cos, sin = cos_ref[...], sin_ref[...]
        lane = lax.broadcasted_iota(jnp.int32, cos.shape, 1)
        first_half = (lane % HEAD_DIM) < (HEAD_DIM // 2)
        for s in range(rot_slabs):
            ys = y[:, s * LANES:(s + 1) * LANES]
            partner = jnp.where(first_half,
                                pltpu.roll(ys, LANES - HEAD_DIM // 2, 1),
                                pltpu.roll(ys, HEAD_DIM // 2, 1))
            out_ref[:, s * LANES:(s + 1) * LANES] = (ys * cos + partner * sin).astype(out_ref.dtype)
    if rot_slabs * LANES < y.shape[1]:
        out_ref[:, rot_slabs * LANES:] = y[:, rot_slabs * LANES:].astype(out_ref.dtype)


def _proj(h, g, w, cos, sin, rot_slabs, name):
    n, d = h.shape
    m = w.shape[1]
    tiles_per_seq = cos.shape[0] // TOKEN_TILE
    table = pl.BlockSpec((TOKEN_TILE, LANES), lambda i: (i % tiles_per_seq, 0))
    return pl.pallas_call(
        functools.partial(_proj_kernel, rot_slabs=rot_slabs),
        out_shape=jax.ShapeDtypeStruct((n, m), BF16),
        grid=(n // TOKEN_TILE,),
        in_specs=[pl.BlockSpec((TOKEN_TILE, d), lambda i: (i, 0)), _resident((1, d)), _resident(w.shape),
                  table, table],
        out_specs=pl.BlockSpec((TOKEN_TILE, m), lambda i: (i, 0)),
        compiler_params=_params(("parallel",)),
        name=name,
    )(h, g.reshape(1, d), w, cos, sin)


def _rotary_tables(seq):
    half = HEAD_DIM // 2
    inv_freq = ROPE_THETA ** (-jnp.arange(half, dtype=F32) / half)
    ang = jnp.arange(seq, dtype=F32)[:, None] * inv_freq[None, :]
    cos, sin = jnp.cos(ang), jnp.sin(ang)
    cos_head = jnp.concatenate([cos, cos], axis=-1)
    sin_head = jnp.concatenate([-sin, sin], axis=-1)
    return jnp.tile(cos_head, (1, HEADS_PER_SLAB)), jnp.tile(sin_head, (1, HEADS_PER_SLAB))


def _sb_kernel(q_ref, k_ref, v_ref, tri_ref, o_ref, acc_ref, carry_ref):
    i = pl.program_id(2)
    t = SB_TILE
    q = q_ref[...] * ATTN_SCALE
    lane = lax.broadcasted_iota(jnp.int32, (t, LANES), 1)
    head0 = lane < HEAD_DIM
    zero = jnp.zeros((t, LANES), BF16)

    def sweep(j, diagonal):
        kb = k_ref[pl.ds(pl.multiple_of(j * t, t), t), :]
        vb = v_ref[pl.ds(pl.multiple_of(j * t, t), t), :]
        k2 = jnp.concatenate([jnp.where(head0, kb, zero), jnp.where(head0, zero, kb)], axis=0)
        v2 = jnp.concatenate([jnp.where(head0, vb, zero), jnp.where(head0, zero, vb)], axis=0)
        z = lax.dot_general(q, k2, (((1,), (1,)), ((), ())), preferred_element_type=F32)
        sp = jnp.log1p(jnp.exp(-jnp.abs(z)))
        log_beta = jnp.minimum(z, 0.0) - sp
        log_1m = -jnp.maximum(z, 0.0) - sp
        if diagonal:
            row = lax.broadcasted_iota(jnp.int32, (t, t), 0)
            col = lax.broadcasted_iota(jnp.int32, (t, t), 1)
            strict = jnp.concatenate([col < row, col < row], axis=1)
            log_1m = jnp.where(strict, log_1m, 0.0)
        ws = []
        for h in range(HEADS_PER_SLAB):
            l_h = log_1m[:, h * t:(h + 1) * t]
            hi = l_h.astype(BF16)
            lo = (l_h - hi.astype(F32)).astype(BF16)
            suffix = (jnp.dot(hi, tri_ref[...], preferred_element_type=F32)
                      + jnp.dot(lo, tri_ref[...], preferred_element_type=F32))
            carry = carry_ref[h]
            ws.append(jnp.exp(log_beta[:, h * t:(h + 1) * t] + suffix + carry))
            carry_ref[h] = carry + jnp.sum(l_h, axis=1, keepdims=True)
        w = jnp.concatenate(ws, axis=1)
        if diagonal:
            w = jnp.where(strict, w, 0.0)
        acc_ref[...] += jnp.dot(w.astype(BF16), v2, preferred_element_type=F32)
        return jnp.max(carry_ref[...])

    acc_ref[...] = jnp.zeros_like(acc_ref)
    carry_ref[...] = jnp.zeros_like(carry_ref)
    top = sweep(i, diagonal=True)

    def cond(state):
        j, top = state
        return jnp.logical_and(j >= 0, top > F32_EXP_UNDERFLOW)

    def body(state):
        j, _ = state
        return j - 1, sweep(j, diagonal=False)

    lax.while_loop(cond, body, (i - 1, top))
    o_ref[...] = acc_ref[...].astype(o_ref.dtype)


def _sb_attention(qkv, batch, seq):
    n = qkv.shape[0]
    slabs = SB_HEADS // HEADS_PER_SLAB
    q_tiles = seq // SB_TILE
    row = lax.broadcasted_iota(jnp.int32, (SB_TILE, SB_TILE), 0)
    col = lax.broadcasted_iota(jnp.int32, (SB_TILE, SB_TILE), 1)
    tri = (row > col).astype(BF16)
    return pl.pallas_call(
        _sb_kernel,
        out_shape=jax.ShapeDtypeStruct((n, SB_HEADS * HEAD_DIM), BF16),
        grid=(batch, slabs, q_tiles),
        in_specs=[
            pl.BlockSpec((SB_TILE, LANES), lambda b, s, i: (b * q_tiles + i, s)),
            pl.BlockSpec((seq, LANES), lambda b, s, i: (b, slabs + s)),
            pl.BlockSpec((seq, LANES), lambda b, s, i: (b, 2 * slabs + s)),
            _resident((SB_TILE, SB_TILE)),
        ],
        out_specs=pl.BlockSpec((SB_TILE, LANES), lambda b, s, i: (b * q_tiles + i, s)),
        scratch_shapes=[pltpu.VMEM((SB_TILE, LANES), F32),
                        pltpu.VMEM((HEADS_PER_SLAB, SB_TILE, 1), F32)],
        compiler_params=_params(("parallel", "parallel", "arbitrary")),
        name="sb_attention",
    )(qkv, qkv, qkv, tri)


def _swa_kernel(sink_ref, q_ref, kp_ref, kc_ref, vp_ref, vc_ref, o_ref):
    nblk = pl.program_id(1)
    w = WINDOW
    qi = lax.broadcasted_iota(jnp.int32, (w, 2 * w), 0)
    ki = lax.broadcasted_iota(jnp.int32, (w, 2 * w), 1)
    diff = qi + w - ki
    valid = (diff >= 0) & (diff < w) & ((nblk > 0) | (ki >= w))
    q = q_ref[...] * ATTN_SCALE
    k = jnp.concatenate([kp_ref[...], kc_ref[...]], axis=0)
    v = jnp.concatenate([vp_ref[...], vc_ref[...]], axis=0)
    for h in range(SWA_KV_HEADS):
        kh = k[:, h * HEAD_DIM:(h + 1) * HEAD_DIM]
        vh = v[:, h * HEAD_DIM:(h + 1) * HEAD_DIM]
        for g in range(SWA_GROUP):
            head = h * SWA_GROUP + g
            qh = q[:, head * HEAD_DIM:(head + 1) * HEAD_DIM]
            s = lax.dot_general(qh, kh, (((1,), (1,)), ((), ())), preferred_element_type=F32)
            s = jnp.where(valid, s, -jnp.inf)
            sink = sink_ref[head]
            m = jnp.maximum(jnp.max(s, axis=1, keepdims=True), sink)
            e = jnp.exp(s - m)
            denom = jnp.sum(e, axis=1, keepdims=True) + jnp.exp(sink - m)
            p = e / denom
            o = jnp.dot(p.astype(BF16), vh, preferred_element_type=F32)
            o_ref[:, head * HEAD_DIM:(head + 1) * HEAD_DIM] = o.astype(o_ref.dtype)


def _swa_attention(q, kv, sinks, batch, seq):
    n, dq = q.shape
    nb = seq // WINDOW
    dkv = SWA_KV_HEADS * HEAD_DIM

    def prev(b, i):
        return (b * nb + jnp.maximum(i - 1, 0), 0)

    def cur(b, i):
        return (b * nb + i, 0)

    return pl.pallas_call(
        _swa_kernel,
        out_shape=jax.ShapeDtypeStruct((n, dq), BF16),
        grid=(batch, nb),
        in_specs=[
            pl.BlockSpec(memory_space=pltpu.SMEM),
            pl.BlockSpec((WINDOW, dq), cur),
            pl.BlockSpec((WINDOW, dkv), prev),
            pl.BlockSpec((WINDOW, dkv), cur),
            pl.BlockSpec((WINDOW, dkv), lambda b, i: (prev(b, i)[0], 1)),
            pl.BlockSpec((WINDOW, dkv), lambda b, i: (cur(b, i)[0], 1)),
        ],
        out_specs=pl.BlockSpec((WINDOW, dq), cur),
        compiler_params=_params(("parallel", "arbitrary")),
        name="swa_attention",
    )(sinks, q, kv, kv, kv, kv)


def kernel(x, ffn1_norm, ffn1_w_in, ffn1_w_out, mix_norm, ffn2_norm, ffn2_w_in, ffn2_w_out,
           sb_w_qkv, sb_w_o, kv_norm, kv_w, swa_w_q, swa_sinks, swa_w_o, final_norm):
    batch, seq, d = x.shape
    n = batch * seq
    cos, sin = _rotary_tables(seq)
    bf = lambda a: a.astype(BF16)
    h = x.reshape(n, d)

    h = _ffn(h, ffn1_norm[0], bf(ffn1_w_in[0]), bf(ffn1_w_out[0]), name="ffn1_l0")
    qkv = _proj(h, mix_norm[0], bf(sb_w_qkv[0]), cos, sin, rot_slabs=0, name="proj_qkv")
    o = _sb_attention(qkv, batch, seq)
    h = _ffn(h, ffn2_norm[0], bf(ffn2_w_in[0]), bf(ffn2_w_out[0]), attn=(o, bf(sb_w_o[0])), name="ffn2_l0")
    k_slabs = SWA_KV_HEADS * HEAD_DIM // LANES
    kv = _proj(h, kv_norm, bf(kv_w), cos, sin, rot_slabs=k_slabs, name="proj_kv")

    h = _ffn(h, ffn1_norm[1], bf(ffn1_w_in[1]), bf(ffn1_w_out[1]), name="ffn1_l1")
    q = _proj(h, mix_norm[1], bf(swa_w_q[0]), cos, sin, rot_slabs=SWA_Q_HEADS * HEAD_DIM // LANES,
              name="proj_q")
    o = _swa_attention(q, kv, swa_sinks[0], batch, seq)
    h = _ffn(h, ffn2_norm[1], bf(ffn2_w_in[1]), bf(ffn2_w_out[1]), attn=(o, bf(swa_w_o[0])),
             final_g=final_norm, name="ffn2_l1")
    return h.reshape(batch, seq, d)
```

```python
import functools

import jax
import jax.numpy as jnp
from jax import lax
from jax.experimental import pallas as pl
from jax.experimental.pallas import tpu as pltpu

F32 = jnp.float32
BF16 = jnp.bfloat16

D_MODEL = 1024
HEAD_DIM = 64
SB_HEADS = 16
SWA_Q_HEADS = 16
SWA_KV_HEADS = 4
SWA_GROUP = SWA_Q_HEADS // SWA_KV_HEADS
WINDOW = 128
D_FF = 2816
ROPE_THETA = 10000.0
RMS_EPS = 1e-6
FFN_RES_SCALE = 0.5
ATTN_SCALE = HEAD_DIM ** -0.5
LOG2E = 1.4426950408889634

LANES = 128
HEADS_PER_SLAB = LANES // HEAD_DIM
VMEM_LIMIT_BYTES = 56 * 1024 * 1024

TOKEN_TILE = 512
FF_CHUNKS = 2
SB_TILE = 128
SB_BAND_TILES = 3
SB_INTERLEAVE = 4
F32_EXP_UNDERFLOW = -104.0


def _rms(x, g):
    return x * lax.rsqrt(jnp.mean(x * x, axis=-1, keepdims=True) + RMS_EPS) * g


def _params(semantics):
    return pltpu.CompilerParams(dimension_semantics=semantics, vmem_limit_bytes=VMEM_LIMIT_BYTES)


def _resident(shape):
    return pl.BlockSpec(shape, lambda *_: (0,) * len(shape), pipeline_mode=pl.Buffered(1))


def _ffn_kernel(*refs, has_attn, has_final_norm):
    refs = list(refs)
    h_ref = refs.pop(0)
    if has_attn:
        o_ref, wo_ref = refs.pop(0), refs.pop(0)
    g_ref, win_ref, wout_ref = refs.pop(0), refs.pop(0), refs.pop(0)
    if has_final_norm:
        fg_ref = refs.pop(0)
    (out_ref,) = refs

    x = h_ref[...]
    if has_attn:
        x = x + jnp.dot(o_ref[...], wo_ref[...], preferred_element_type=F32)
    xn = _rms(x, g_ref[...]).astype(BF16)
    fc = D_FF // FF_CHUNKS
    y = None
    for c in range(FF_CHUNKS):
        gate = jnp.dot(xn, win_ref[:, c * fc:(c + 1) * fc], preferred_element_type=F32)
        up = jnp.dot(xn, win_ref[:, D_FF + c * fc:D_FF + (c + 1) * fc], preferred_element_type=F32)
        act = (gate * jax.nn.sigmoid(gate) * up).astype(BF16)
        yc = jnp.dot(act, wout_ref[c * fc:(c + 1) * fc, :], preferred_element_type=F32)
        y = yc if y is None else y + yc
    x = x + FFN_RES_SCALE * y
    if has_final_norm:
        x = _rms(x, fg_ref[...])
    out_ref[...] = x


def _ffn(h, g, w_in, w_out, attn=None, final_g=None, name="ffn"):
    n, d = h.shape
    row = pl.BlockSpec((TOKEN_TILE, d), lambda i: (i, 0))
    args, specs = [h], [row]
    if attn is not None:
        o, w_o = attn
        args += [o, w_o]
        specs += [row, _resident(w_o.shape)]
    args += [g.reshape(1, d), w_in, w_out]
    specs += [_resident((1, d)), _resident(w_in.shape), _resident(w_out.shape)]
    if final_g is not None:
        args.append(final_g.reshape(1, d))
        specs.append(_resident((1, d)))
    return pl.pallas_call(
        functools.partial(_ffn_kernel, has_attn=attn is not None, has_final_norm=final_g is not None),
        out_shape=jax.ShapeDtypeStruct((n, d), F32),
        grid=(n // TOKEN_TILE,),
        in_specs=specs,
        out_specs=row,
        compiler_params=_params(("parallel",)),
        name=name,
    )(*args)


def _proj_kernel(h_ref, g_ref, w_ref, cos_ref, sin_ref, out_ref, *, rot_slabs, dup_heads):
    xn = _rms(h_ref[...], g_ref[...]).astype(BF16)
    y = jnp.dot(xn, w_ref[...], preferred_element_type=F32)
    cos, sin = cos_ref[...], sin_ref[...]
    lane = lax.broadcasted_iota(jnp.int32, cos.shape, 1)
    first_half = (lane % HEAD_DIM) < (HEAD_DIM // 2)
    head0 = lane < HEAD_DIM
    for s in range(y.shape[1] // LANES):
        ys = y[:, s * LANES:(s + 1) * LANES]
        if s < rot_slabs:
            partner = jnp.where(first_half,
                                pltpu.roll(ys, LANES - HEAD_DIM // 2, 1),
                                pltpu.roll(ys, HEAD_DIM // 2, 1))
            ys = ys * cos + partner * sin
        if dup_heads:
            swapped = pltpu.roll(ys, HEAD_DIM, 1)
            out_ref[:, 2 * s * LANES:(2 * s + 1) * LANES] = jnp.where(head0, ys, swapped).astype(out_ref.dtype)
            out_ref[:, (2 * s + 1) * LANES:(2 * s + 2) * LANES] = jnp.where(head0, swapped, ys).astype(out_ref.dtype)
        else:
            out_ref[:, s * LANES:(s + 1) * LANES] = ys.astype(out_ref.dtype)


def _proj(h, g, w, cos, sin, rot_slabs, name, dup_heads=False):
    n, d = h.shape
    m = w.shape[1] * (HEADS_PER_SLAB if dup_heads else 1)
    tiles_per_seq = cos.shape[0] // TOKEN_TILE
    table = pl.BlockSpec((TOKEN_TILE, LANES), lambda i: (i % tiles_per_seq, 0))
    return pl.pallas_call(
        functools.partial(_proj_kernel, rot_slabs=rot_slabs, dup_heads=dup_heads),
        out_shape=jax.ShapeDtypeStruct((n, m), BF16),
        grid=(n // TOKEN_TILE,),
        in_specs=[pl.BlockSpec((TOKEN_TILE, d), lambda i: (i, 0)), _resident((1, d)), _resident(w.shape),
                  table, table],
        out_specs=pl.BlockSpec((TOKEN_TILE, m), lambda i: (i, 0)),
        compiler_params=_params(("parallel",)),
        name=name,
    )(h, g.reshape(1, d), w, cos, sin)


def _rotary_tables(seq):
    half = HEAD_DIM // 2
    inv_freq = ROPE_THETA ** (-jnp.arange(half, dtype=F32) / half)
    ang = jnp.arange(seq, dtype=F32)[:, None] * inv_freq[None, :]
    cos, sin = jnp.cos(ang), jnp.sin(ang)
    cos_head = jnp.concatenate([cos, cos], axis=-1)
    sin_head = jnp.concatenate([-sin, sin], axis=-1)
    return jnp.tile(cos_head, (1, HEADS_PER_SLAB)), jnp.tile(sin_head, (1, HEADS_PER_SLAB))


def _sb_kernel(q_ref, k_ref, v_ref, tri_ref, o_ref, qs_ref, acc_ref, carry_ref):
    t = SB_TILE
    q_tiles = q_ref.shape[0] // t
    lane = lax.broadcasted_iota(jnp.int32, (t, LANES), 1)
    head0 = lane < HEAD_DIM
    row = lax.broadcasted_iota(jnp.int32, (2 * t, t), 0) & (t - 1)
    col = lax.broadcasted_iota(jnp.int32, (2 * t, t), 1)
    strict = col < row

    def key_rows(first_tile, ntiles):
        return pl.ds(pl.multiple_of(first_tile * t, t), ntiles * t)

    def logits_phase(slot, first_tile, ntiles, diagonal):
        z = lax.dot_general(qs_ref[slot], k_ref[key_rows(first_tile, ntiles), :], (((1,), (1,)), ((), ())),
                            preferred_element_type=F32)
        sp = jnp.log(1.0 + jnp.exp2(jnp.abs(z) * -LOG2E))
        log_beta = jnp.minimum(z, 0.0) - sp
        log_1m = log_beta - z
        split, totals = [], []
        for k in range(ntiles):
            l_k = log_1m[:, k * t:(k + 1) * t]
            if diagonal and k == ntiles - 1:
                l_k = jnp.where(strict, l_k, 0.0)
            hi = l_k.astype(BF16)
            lo = (l_k - hi.astype(F32)).astype(BF16)
            split.append(jnp.concatenate([hi, lo], axis=1))
            totals.append(jnp.sum(l_k, axis=1, keepdims=True))
        return log_beta, jnp.concatenate(split, axis=0), totals

    def suffix_phase(split):
        return jnp.dot(split, tri_ref[...], preferred_element_type=F32)

    def weights_phase(slot, first_tile, ntiles, diagonal, log_beta, within, totals):
        later = None if diagonal else carry_ref[slot]
        ws = [None] * ntiles
        for k in reversed(range(ntiles)):
            x = log_beta[:, k * t:(k + 1) * t] + within[k * 2 * t:(k + 1) * 2 * t]
            if later is not None:
                x = x + later
            w_k = jnp.exp(x)
            if diagonal and k == ntiles - 1:
                w_k = jnp.where(strict, w_k, 0.0)
            ws[k] = w_k.astype(BF16)
            later = totals[k] if later is None else later + totals[k]
        w = ws[0] if ntiles == 1 else jnp.concatenate(ws, axis=1)
        pv = jnp.dot(w, v_ref[key_rows(first_tile, ntiles), :], preferred_element_type=F32)
        acc = pv if diagonal else acc_ref[slot] + pv
        acc_ref[slot] = acc
        carry_ref[slot] = later
        return acc, later

    def write_output(i, acc):
        rows = pl.ds(pl.multiple_of(i * t, t), t)
        o_ref[rows, :] = jnp.where(head0, acc[:t], acc[t:]).astype(o_ref.dtype)

    def q_tiles_group(i0, count, ntiles):
        for slot in range(count):
            rows = pl.ds(pl.multiple_of((i0 + slot) * t, t), t)
            q = q_ref[rows, :] * ATTN_SCALE
            zero = jnp.zeros_like(q)
            qs_ref[slot] = jnp.concatenate([jnp.where(head0, q, zero), jnp.where(head0, zero, q)], axis=0)
        firsts = [i0 + slot - (ntiles - 1) for slot in range(count)]
        logits = [logits_phase(slot, firsts[slot], ntiles, True) for slot in range(count)]
        within = [suffix_phase(logits[slot][1]) for slot in range(count)]
        top = None
        for slot in range(count):
            acc, carry = weights_phase(slot, firsts[slot], ntiles, True, logits[slot][0], within[slot],
                                       logits[slot][2])
            write_output(i0 + slot, acc)
            top = carry if top is None else jnp.maximum(top, carry)

        @pl.when(jnp.max(top) > F32_EXP_UNDERFLOW)
        def _():
            for slot in range(count):
                def cond(state):
                    j, top = state
                    return jnp.logical_and(j >= 0, top > F32_EXP_UNDERFLOW)

                def body(state, slot=slot):
                    j, _ = state
                    log_beta, split, totals = logits_phase(slot, j, 1, False)
                    _, carry = weights_phase(slot, j, 1, False, log_beta, suffix_phase(split), totals)
                    return j - 1, jnp.max(carry)

                lax.while_loop(cond, body, (firsts[slot] - 1, jnp.max(carry_ref[slot])))
                write_output(i0 + slot, acc_ref[slot])

    def head_tile(i, c):
        q_tiles_group(i, 1, 1)
        return c

    def band_group(g, c):
        q_tiles_group(SB_BAND_TILES - 1 + g * SB_INTERLEAVE, SB_INTERLEAVE, SB_BAND_TILES)
        return c

    band_tiles = q_tiles - (SB_BAND_TILES - 1)
    lax.fori_loop(0, SB_BAND_TILES - 1, head_tile, 0)
    lax.fori_loop(0, band_tiles // SB_INTERLEAVE, band_group, 0)
    if band_tiles % SB_INTERLEAVE:
        q_tiles_group(q_tiles - band_tiles % SB_INTERLEAVE, band_tiles % SB_INTERLEAVE, SB_BAND_TILES)


def _suffix_matrix(n):
    row = lax.broadcasted_iota(jnp.int32, (n, n), 0)
    col = lax.broadcasted_iota(jnp.int32, (n, n), 1)
    tri = (row > col).astype(BF16)
    return jnp.concatenate([tri, tri], axis=0)


def _sb_attention(qkv, batch, seq):
    n = qkv.shape[0]
    slabs = SB_HEADS // HEADS_PER_SLAB
    tri = _suffix_matrix(SB_TILE)
    return pl.pallas_call(
        _sb_kernel,
        out_shape=jax.ShapeDtypeStruct((n, SB_HEADS * HEAD_DIM), BF16),
        grid=(batch, slabs),
        in_specs=[
            pl.BlockSpec((seq, LANES), lambda b, s: (b, s)),
            pl.BlockSpec((seq, LANES), lambda b, s: (b, slabs + s)),
            pl.BlockSpec((seq, LANES), lambda b, s: (b, 2 * slabs + s)),
            _resident(tri.shape),
        ],
        out_specs=pl.BlockSpec((seq, LANES), lambda b, s: (b, s)),
        scratch_shapes=[pltpu.VMEM((SB_INTERLEAVE, 2 * SB_TILE, LANES), BF16),
                        pltpu.VMEM((SB_INTERLEAVE, 2 * SB_TILE, LANES), F32),
                        pltpu.VMEM((SB_INTERLEAVE, 2 * SB_TILE, 1), F32)],
        compiler_params=_params(("parallel", "parallel")),
        name="sb_attention",
    )(qkv, qkv, qkv, tri)


def _swa_kernel(sink_ref, q_ref, kvp_ref, kvc_ref, o_ref):
    nblk = pl.program_id(1)
    w = WINDOW
    rows = SWA_GROUP * w
    qi = lax.broadcasted_iota(jnp.int32, (rows, 2 * w), 0) & (w - 1)
    ki = lax.broadcasted_iota(jnp.int32, (rows, 2 * w), 1)
    diff = qi + w - ki
    valid = (diff >= 0) & (diff < w) & ((nblk > 0) | (ki >= w))
    lane = lax.broadcasted_iota(jnp.int32, (w, LANES), 1)
    head0 = lane < HEAD_DIM
    zero = jnp.zeros((w, LANES), BF16)
    ones = jnp.ones((2 * w, LANES), BF16)
    for h in range(SWA_KV_HEADS):
        k_cols = slice(h * LANES, (h + 1) * LANES)
        v_cols = slice((SWA_KV_HEADS + h) * LANES, (SWA_KV_HEADS + h + 1) * LANES)
        kk = jnp.concatenate([kvp_ref[:, k_cols], kvc_ref[:, k_cols]], axis=0)
        vv = jnp.concatenate([kvp_ref[:, v_cols], kvc_ref[:, v_cols]], axis=0)
        stacked, sinks = [], []
        for g in range(SWA_GROUP):
            head = h * SWA_GROUP + g
            slab = head // HEADS_PER_SLAB
            q = q_ref[:, slab * LANES:(slab + 1) * LANES] * ATTN_SCALE
            stacked.append(jnp.where(head0, q, zero) if head % HEADS_PER_SLAB == 0 else jnp.where(head0, zero, q))
            sinks.append(jnp.full((w, LANES), sink_ref[head], F32))
        qs = jnp.concatenate(stacked, axis=0)
        sink = jnp.concatenate(sinks, axis=0)
        s = lax.dot_general(qs, kk, (((1,), (1,)), ((), ())), preferred_element_type=F32)
        s = jnp.where(valid, s, -jnp.inf)
        m = jnp.maximum(jnp.broadcast_to(jnp.max(s, axis=1, keepdims=True), sink.shape), sink)
        e = jnp.exp(s - jnp.concatenate([m, m], axis=1)).astype(BF16)
        denom = jnp.dot(e, ones, preferred_element_type=F32) + jnp.exp(sink - m)
        o = jnp.dot(e, vv, preferred_element_type=F32) * (1.0 / denom)
        for pair in range(SWA_GROUP // HEADS_PER_SLAB):
            slab = (h * SWA_GROUP) // HEADS_PER_SLAB + pair
            lo = o[(2 * pair) * w:(2 * pair + 1) * w]
            hi = o[(2 * pair + 1) * w:(2 * pair + 2) * w]
            o_ref[:, slab * LANES:(slab + 1) * LANES] = jnp.where(head0, lo, hi).astype(o_ref.dtype)


def _swa_attention(q, kv, sinks, batch, seq):
    n, dq = q.shape
    nb = seq // WINDOW
    dkv = kv.shape[1]
    return pl.pallas_call(
        _swa_kernel,
        out_shape=jax.ShapeDtypeStruct((n, dq), BF16),
        grid=(batch, nb),
        in_specs=[
            pl.BlockSpec(memory_space=pltpu.SMEM),
            pl.BlockSpec((WINDOW, dq), lambda b, i: (b * nb + i, 0)),
            pl.BlockSpec((WINDOW, dkv), lambda b, i: (b * nb + jnp.maximum(i - 1, 0), 0)),
            pl.BlockSpec((WINDOW, dkv), lambda b, i: (b * nb + i, 0)),
        ],
        out_specs=pl.BlockSpec((WINDOW, dq), lambda b, i: (b * nb + i, 0)),
        compiler_params=_params(("parallel", "arbitrary")),
        name="swa_attention",
    )(sinks, q, kv, kv)


def kernel(x, ffn1_norm, ffn1_w_in, ffn1_w_out, mix_norm, ffn2_norm, ffn2_w_in, ffn2_w_out,
           sb_w_qkv, sb_w_o, kv_norm, kv_w, swa_w_q, swa_sinks, swa_w_o, final_norm):
    batch, seq, d = x.shape
    n = batch * seq
    cos, sin = _rotary_tables(seq)
    bf = lambda a: a.astype(BF16)
    h = x.reshape(n, d)

    h = _ffn(h, ffn1_norm[0], bf(ffn1_w_in[0]), bf(ffn1_w_out[0]), name="ffn1_l0")
    qkv = _proj(h, mix_norm[0], bf(sb_w_qkv[0]), cos, sin, rot_slabs=0, name="proj_qkv")
    o = _sb_attention(qkv, batch, seq)
    h = _ffn(h, ffn2_norm[0], bf(ffn2_w_in[0]), bf(ffn2_w_out[0]), attn=(o, bf(sb_w_o[0])), name="ffn2_l0")
    k_slabs = SWA_KV_HEADS * HEAD_DIM // LANES
    kv = _proj(h, kv_norm, bf(kv_w), cos, sin, rot_slabs=k_slabs, name="proj_kv", dup_heads=True)

    h = _ffn(h, ffn1_norm[1], bf(ffn1_w_in[1]), bf(ffn1_w_out[1]), name="ffn1_l1")
    q = _proj(h, mix_norm[1], bf(swa_w_q[0]), cos, sin, rot_slabs=SWA_Q_HEADS * HEAD_DIM // LANES,
              name="proj_q")
    o = _swa_attention(q, kv, swa_sinks[0], batch, seq)
    h = _ffn(h, ffn2_norm[1], bf(ffn2_w_in[1]), bf(ffn2_w_out[1]), attn=(o, bf(swa_w_o[0])),
             final_g=final_norm, name="ffn2_l1")
    return h.reshape(batch, seq, d)
```

```python
import functools

import jax
import jax.numpy as jnp
from jax import lax
from jax.experimental import pallas as pl
from jax.experimental.pallas import tpu as pltpu

F32 = jnp.float32
BF16 = jnp.bfloat16

D_MODEL = 1024
HEAD_DIM = 64
SB_HEADS = 16
SWA_Q_HEADS = 16
SWA_KV_HEADS = 4
SWA_GROUP = SWA_Q_HEADS // SWA_KV_HEADS
WINDOW = 128
D_FF = 2816
ROPE_THETA = 10000.0
RMS_EPS = 1e-6
FFN_RES_SCALE = 0.5
ATTN_SCALE = HEAD_DIM ** -0.5
LOG2E = 1.4426950408889634

LANES = 128
HEADS_PER_SLAB = LANES // HEAD_DIM
VMEM_LIMIT_BYTES = 56 * 1024 * 1024

TOKEN_TILE = 512
FFN_TILE = 1024
FF_CHUNKS = 11
SB_TILE = 128
SB_BAND_TILES = 3
SB_INTERLEAVE = 4
F32_EXP_UNDERFLOW = -104.0


def _rms(x, g):
    return x * lax.rsqrt(jnp.mean(x * x, axis=-1, keepdims=True) + RMS_EPS) * g


def _params(semantics):
    return pltpu.CompilerParams(dimension_semantics=semantics, vmem_limit_bytes=VMEM_LIMIT_BYTES)


def _resident(shape):
    return pl.BlockSpec(shape, lambda *_: (0,) * len(shape), pipeline_mode=pl.Buffered(1))


def _ffn_kernel(*refs, has_attn, has_final_norm):
    refs = list(refs)
    h_ref = refs.pop(0)
    if has_attn:
        o_ref, wo_ref = refs.pop(0), refs.pop(0)
    g_ref, win_ref, wout_ref = refs.pop(0), refs.pop(0), refs.pop(0)
    if has_final_norm:
        fg_ref = refs.pop(0)
    (out_ref,) = refs

    x = h_ref[...]
    if has_attn:
        x = x + jnp.dot(o_ref[...], wo_ref[...], preferred_element_type=F32)
    xn = _rms(x, g_ref[...]).astype(BF16)
    fc = D_FF // FF_CHUNKS
    y = None
    for c in range(FF_CHUNKS):
        gate = jnp.dot(xn, win_ref[:, c * fc:(c + 1) * fc], preferred_element_type=F32)
        up = jnp.dot(xn, win_ref[:, D_FF + c * fc:D_FF + (c + 1) * fc], preferred_element_type=F32)
        act = (gate * jax.nn.sigmoid(gate) * up).astype(BF16)
        yc = jnp.dot(act, wout_ref[c * fc:(c + 1) * fc, :], preferred_element_type=F32)
        y = yc if y is None else y + yc
    x = x + FFN_RES_SCALE * y
    if has_final_norm:
        x = _rms(x, fg_ref[...])
    out_ref[...] = x


def _ffn(h, g, w_in, w_out, attn=None, final_g=None, name="ffn"):
    n, d = h.shape
    row = pl.BlockSpec((FFN_TILE, d), lambda i: (i, 0))
    args, specs = [h], [row]
    if attn is not None:
        o, w_o = attn
        args += [o, w_o]
        specs += [row, _resident(w_o.shape)]
    args += [g.reshape(1, d), w_in, w_out]
    specs += [_resident((1, d)), _resident(w_in.shape), _resident(w_out.shape)]
    if final_g is not None:
        args.append(final_g.reshape(1, d))
        specs.append(_resident((1, d)))
    return pl.pallas_call(
        functools.partial(_ffn_kernel, has_attn=attn is not None, has_final_norm=final_g is not None),
        out_shape=jax.ShapeDtypeStruct((n, d), F32),
        grid=(n // FFN_TILE,),
        in_specs=specs,
        out_specs=row,
        compiler_params=_params(("parallel",)),
        name=name,
    )(*args)


def _proj_kernel(h_ref, g_ref, w_ref, cos_ref, sin_ref, out_ref, *, rot_slabs, dup_heads):
    xn = _rms(h_ref[...], g_ref[...]).astype(BF16)
    y = jnp.dot(xn, w_ref[...], preferred_element_type=F32)
    cos, sin = cos_ref[...], sin_ref[...]
    lane = lax.broadcasted_iota(jnp.int32, cos.shape, 1)
    first_half = (lane % HEAD_DIM) < (HEAD_DIM // 2)
    head0 = lane < HEAD_DIM
    for s in range(y.shape[1] // LANES):
        ys = y[:, s * LANES:(s + 1) * LANES]
        if s < rot_slabs:
            partner = jnp.where(first_half,
                                pltpu.roll(ys, LANES - HEAD_DIM // 2, 1),
                                pltpu.roll(ys, HEAD_DIM // 2, 1))
            ys = ys * cos + partner * sin
        if dup_heads:
            swapped = pltpu.roll(ys, HEAD_DIM, 1)
            out_ref[:, 2 * s * LANES:(2 * s + 1) * LANES] = jnp.where(head0, ys, swapped).astype(out_ref.dtype)
            out_ref[:, (2 * s + 1) * LANES:(2 * s + 2) * LANES] = jnp.where(head0, swapped, ys).astype(out_ref.dtype)
        else:
            out_ref[:, s * LANES:(s + 1) * LANES] = ys.astype(out_ref.dtype)


def _proj(h, g, w, cos, sin, rot_slabs, name, dup_heads=False):
    n, d = h.shape
    m = w.shape[1] * (HEADS_PER_SLAB if dup_heads else 1)
    tiles_per_seq = cos.shape[0] // TOKEN_TILE
    table = pl.BlockSpec((TOKEN_TILE, LANES), lambda i: (i % tiles_per_seq, 0))
    return pl.pallas_call(
        functools.partial(_proj_kernel, rot_slabs=rot_slabs, dup_heads=dup_heads),
        out_shape=jax.ShapeDtypeStruct((n, m), BF16),
        grid=(n // TOKEN_TILE,),
        in_specs=[pl.BlockSpec((TOKEN_TILE, d), lambda i: (i, 0)), _resident((1, d)), _resident(w.shape),
                  table, table],
        out_specs=pl.BlockSpec((TOKEN_TILE, m), lambda i: (i, 0)),
        compiler_params=_params(("parallel",)),
        name=name,
    )(h, g.reshape(1, d), w, cos, sin)


def _rotary_tables(seq):
    half = HEAD_DIM // 2
    inv_freq = ROPE_THETA ** (-jnp.arange(half, dtype=F32) / half)
    ang = jnp.arange(seq, dtype=F32)[:, None] * inv_freq[None, :]
    cos, sin = jnp.cos(ang), jnp.sin(ang)
    cos_head = jnp.concatenate([cos, cos], axis=-1)
    sin_head = jnp.concatenate([-sin, sin], axis=-1)
    return jnp.tile(cos_head, (1, HEADS_PER_SLAB)), jnp.tile(sin_head, (1, HEADS_PER_SLAB))


def _sb_kernel(q_ref, k_ref, v_ref, tri_ref, o_ref, qs_ref, acc_ref, carry_ref):
    t = SB_TILE
    q_tiles = q_ref.shape[0] // t
    lane = lax.broadcasted_iota(jnp.int32, (t, LANES), 1)
    head0 = lane < HEAD_DIM
    row = lax.broadcasted_iota(jnp.int32, (2 * t, t), 0) & (t - 1)
    col = lax.broadcasted_iota(jnp.int32, (2 * t, t), 1)
    strict = col < row

    def key_rows(first_tile, ntiles):
        return pl.ds(pl.multiple_of(first_tile * t, t), ntiles * t)

    def logits_phase(slot, first_tile, ntiles, diagonal):
        z = lax.dot_general(qs_ref[slot], k_ref[key_rows(first_tile, ntiles), :], (((1,), (1,)), ((), ())),
                            preferred_element_type=F32)
        sp = jnp.log(1.0 + jnp.exp2(jnp.abs(z) * -LOG2E))
        log_beta = jnp.minimum(z, 0.0) - sp
        log_1m = log_beta - z
        split, totals = [], []
        for k in range(ntiles):
            l_k = log_1m[:, k * t:(k + 1) * t]
            if diagonal and k == ntiles - 1:
                l_k = jnp.where(strict, l_k, 0.0)
            hi = l_k.astype(BF16)
            lo = (l_k - hi.astype(F32)).astype(BF16)
            split.append(jnp.concatenate([hi, lo], axis=1))
            totals.append(jnp.sum(l_k, axis=1, keepdims=True))
        return log_beta, jnp.concatenate(split, axis=0), totals

    def suffix_phase(split):
        return jnp.dot(split, tri_ref[...], preferred_element_type=F32)

    def weights_phase(slot, first_tile, ntiles, diagonal, log_beta, within, totals):
        later = None if diagonal else carry_ref[slot]
        ws = [None] * ntiles
        for k in reversed(range(ntiles)):
            x = log_beta[:, k * t:(k + 1) * t] + within[k * 2 * t:(k + 1) * 2 * t]
            if later is not None:
                x = x + later
            w_k = jnp.exp(x)
            if diagonal and k == ntiles - 1:
                w_k = jnp.where(strict, w_k, 0.0)
            ws[k] = w_k.astype(BF16)
            later = totals[k] if later is None else later + totals[k]
        w = ws[0] if ntiles == 1 else jnp.concatenate(ws, axis=1)
        pv = jnp.dot(w, v_ref[key_rows(first_tile, ntiles), :], preferred_element_type=F32)
        acc = pv if diagonal else acc_ref[slot] + pv
        acc_ref[slot] = acc
        carry_ref[slot] = later
        return acc, later

    def write_output(i, acc):
        rows = pl.ds(pl.multiple_of(i * t, t), t)
        o_ref[rows, :] = jnp.where(head0, acc[:t], acc[t:]).astype(o_ref.dtype)

    def q_tiles_group(i0, count, ntiles):
        for slot in range(count):
            rows = pl.ds(pl.multiple_of((i0 + slot) * t, t), t)
            q = q_ref[rows, :] * ATTN_SCALE
            zero = jnp.zeros_like(q)
            qs_ref[slot] = jnp.concatenate([jnp.where(head0, q, zero), jnp.where(head0, zero, q)], axis=0)
        firsts = [i0 + slot - (ntiles - 1) for slot in range(count)]
        logits = [logits_phase(slot, firsts[slot], ntiles, True) for slot in range(count)]
        within = [suffix_phase(logits[slot][1]) for slot in range(count)]
        top = None
        for slot in range(count):
            acc, carry = weights_phase(slot, firsts[slot], ntiles, True, logits[slot][0], within[slot],
                                       logits[slot][2])
            write_output(i0 + slot, acc)
            top = carry if top is None else jnp.maximum(top, carry)

        @pl.when(jnp.max(top) > F32_EXP_UNDERFLOW)
        def _():
            for slot in range(count):
                def cond(state):
                    j, top = state
                    return jnp.logical_and(j >= 0, top > F32_EXP_UNDERFLOW)

                def body(state, slot=slot):
                    j, _ = state
                    log_beta, split, totals = logits_phase(slot, j, 1, False)
                    _, carry = weights_phase(slot, j, 1, False, log_beta, suffix_phase(split), totals)
                    return j - 1, jnp.max(carry)

                lax.while_loop(cond, body, (firsts[slot] - 1, jnp.max(carry_ref[slot])))
                write_output(i0 + slot, acc_ref[slot])

    def head_tile(i, c):
        q_tiles_group(i, 1, 1)
        return c

    def band_group(g, c):
        q_tiles_group(SB_BAND_TILES - 1 + g * SB_INTERLEAVE, SB_INTERLEAVE, SB_BAND_TILES)
        return c

    band_tiles = q_tiles - (SB_BAND_TILES - 1)
    lax.fori_loop(0, SB_BAND_TILES - 1, head_tile, 0)
    lax.fori_loop(0, band_tiles // SB_INTERLEAVE, band_group, 0)
    if band_tiles % SB_INTERLEAVE:
        q_tiles_group(q_tiles - band_tiles % SB_INTERLEAVE, band_tiles % SB_INTERLEAVE, SB_BAND_TILES)


def _suffix_matrix(n):
    row = lax.broadcasted_iota(jnp.int32, (n, n), 0)
    col = lax.broadcasted_iota(jnp.int32, (n, n), 1)
    tri = (row > col).astype(BF16)
    return jnp.concatenate([tri, tri], axis=0)


def _sb_attention(qkv, batch, seq):
    n = qkv.shape[0]
    slabs = SB_HEADS // HEADS_PER_SLAB
    tri = _suffix_matrix(SB_TILE)
    return pl.pallas_call(
        _sb_kernel,
        out_shape=jax.ShapeDtypeStruct((n, SB_HEADS * HEAD_DIM), BF16),
        grid=(batch, slabs),
        in_specs=[
            pl.BlockSpec((seq, LANES), lambda b, s: (b, s)),
            pl.BlockSpec((seq, LANES), lambda b, s: (b, slabs + s)),
            pl.BlockSpec((seq, LANES), lambda b, s: (b, 2 * slabs + s)),
            _resident(tri.shape),
        ],
        out_specs=pl.BlockSpec((seq, LANES), lambda b, s: (b, s)),
        scratch_shapes=[pltpu.VMEM((SB_INTERLEAVE, 2 * SB_TILE, LANES), BF16),
                        pltpu.VMEM((SB_INTERLEAVE, 2 * SB_TILE, LANES), F32),
                        pltpu.VMEM((SB_INTERLEAVE, 2 * SB_TILE, 1), F32)],
        compiler_params=_params(("parallel", "parallel")),
        name="sb_attention",
    )(qkv, qkv, qkv, tri)


def _swa_kernel(sink_ref, bias_ref, q_ref, kvp_ref, kvc_ref, o_ref):
    w = WINDOW
    bias = jnp.concatenate([bias_ref[0]] * SWA_GROUP, axis=0)
    lane = lax.broadcasted_iota(jnp.int32, (w, LANES), 1)
    head0 = lane < HEAD_DIM
    zero = jnp.zeros((w, LANES), BF16)
    ones = jnp.ones((2 * w, LANES), BF16)
    for h in range(SWA_KV_HEADS):
        k_cols = slice(h * LANES, (h + 1) * LANES)
        v_cols = slice((SWA_KV_HEADS + h) * LANES, (SWA_KV_HEADS + h + 1) * LANES)
        kk = jnp.concatenate([kvp_ref[:, k_cols], kvc_ref[:, k_cols]], axis=0)
        vv = jnp.concatenate([kvp_ref[:, v_cols], kvc_ref[:, v_cols]], axis=0)
        vv_ones = jnp.concatenate([vv, ones], axis=1)
        stacked, sinks = [], []
        for g in range(SWA_GROUP):
            head = h * SWA_GROUP + g
            slab = head // HEADS_PER_SLAB
            q = q_ref[:, slab * LANES:(slab + 1) * LANES] * ATTN_SCALE
            stacked.append(jnp.where(head0, q, zero) if head % HEADS_PER_SLAB == 0 else jnp.where(head0, zero, q))
            sinks.append(jnp.full((w, LANES), sink_ref[head], F32))
        qs = jnp.concatenate(stacked, axis=0)
        sink = jnp.concatenate(sinks, axis=0)
        s = lax.dot_general(qs, kk, (((1,), (1,)), ((), ())), preferred_element_type=F32) + bias
        m = jnp.maximum(jnp.broadcast_to(jnp.max(s, axis=1, keepdims=True), sink.shape), sink)
        e = jnp.exp(s - jnp.concatenate([m, m], axis=1)).astype(BF16)
        ov = jnp.dot(e, vv_ones, preferred_element_type=F32)
        denom = ov[:, LANES:] + jnp.exp(sink - m)
        o = ov[:, :LANES] * (1.0 / denom)
        for pair in range(SWA_GROUP // HEADS_PER_SLAB):
            slab = (h * SWA_GROUP) // HEADS_PER_SLAB + pair
            lo = o[(2 * pair) * w:(2 * pair + 1) * w]
            hi = o[(2 * pair + 1) * w:(2 * pair + 2) * w]
            o_ref[:, slab * LANES:(slab + 1) * LANES] = jnp.where(head0, lo, hi).astype(o_ref.dtype)


def _swa_attention(q, kv, sinks, batch, seq):
    n, dq = q.shape
    nb = seq // WINDOW
    dkv = kv.shape[1]
    qi = lax.broadcasted_iota(jnp.int32, (WINDOW, 2 * WINDOW), 0)
    ki = lax.broadcasted_iota(jnp.int32, (WINDOW, 2 * WINDOW), 1)
    diff = qi + WINDOW - ki
    in_window = (diff >= 0) & (diff < WINDOW)
    bias = jnp.where(jnp.stack([in_window & (ki >= WINDOW), in_window]), 0.0, -jnp.inf).astype(F32)
    return pl.pallas_call(
        _swa_kernel,
        out_shape=jax.ShapeDtypeStruct((n, dq), BF16),
        grid=(batch, nb),
        in_specs=[
            pl.BlockSpec(memory_space=pltpu.SMEM),
            pl.BlockSpec((1, WINDOW, 2 * WINDOW), lambda b, i: (jnp.minimum(i, 1), 0, 0)),
            pl.BlockSpec((WINDOW, dq), lambda b, i: (b * nb + i, 0)),
            pl.BlockSpec((WINDOW, dkv), lambda b, i: (b * nb + jnp.maximum(i - 1, 0), 0)),
            pl.BlockSpec((WINDOW, dkv), lambda b, i: (b * nb + i, 0)),
        ],
        out_specs=pl.BlockSpec((WINDOW, dq), lambda b, i: (b * nb + i, 0)),
        compiler_params=_params(("parallel", "arbitrary")),
        name="swa_attention",
    )(sinks, bias, q, kv, kv)


def kernel(x, ffn1_norm, ffn1_w_in, ffn1_w_out, mix_norm, ffn2_norm, ffn2_w_in, ffn2_w_out,
           sb_w_qkv, sb_w_o, kv_norm, kv_w, swa_w_q, swa_sinks, swa_w_o, final_norm):
    batch, seq, d = x.shape
    n = batch * seq
    cos, sin = _rotary_tables(seq)
    bf = lambda a: a.astype(BF16)
    h = x.reshape(n, d)

    h = _ffn(h, ffn1_norm[0], bf(ffn1_w_in[0]), bf(ffn1_w_out[0]), name="ffn1_l0")
    qkv = _proj(h, mix_norm[0], bf(sb_w_qkv[0]), cos, sin, rot_slabs=0, name="proj_qkv")
    o = _sb_attention(qkv, batch, seq)
    h = _ffn(h, ffn2_norm[0], bf(ffn2_w_in[0]), bf(ffn2_w_out[0]), attn=(o, bf(sb_w_o[0])), name="ffn2_l0")
    k_slabs = SWA_KV_HEADS * HEAD_DIM // LANES
    kv = _proj(h, kv_norm, bf(kv_w), cos, sin, rot_slabs=k_slabs, name="proj_kv", dup_heads=True)

    h = _ffn(h, ffn1_norm[1], bf(ffn1_w_in[1]), bf(ffn1_w_out[1]), name="ffn1_l1")
    q = _proj(h, mix_norm[1], bf(swa_w_q[0]), cos, sin, rot_slabs=SWA_Q_HEADS * HEAD_DIM // LANES,
              name="proj_q")
    o = _swa_attention(q, kv, swa_sinks[0], batch, seq)
    h = _ffn(h, ffn2_norm[1], bf(ffn2_w_in[1]), bf(ffn2_w_out[1]), attn=(o, bf(swa_w_o[0])),
             final_g=final_norm, name="ffn2_l1")
    return h.reshape(batch, seq, d)
```

```python
import functools

import jax
import jax.numpy as jnp
from jax import lax
from jax.experimental import pallas as pl
from jax.experimental.pallas import tpu as pltpu

F32 = jnp.float32
BF16 = jnp.bfloat16

D_MODEL = 1024
HEAD_DIM = 64
SB_HEADS = 16
SWA_Q_HEADS = 16
SWA_KV_HEADS = 4
SWA_GROUP = SWA_Q_HEADS // SWA_KV_HEADS
WINDOW = 128
D_FF = 2816
ROPE_THETA = 10000.0
RMS_EPS = 1e-6
FFN_RES_SCALE = 0.5
ATTN_SCALE = HEAD_DIM ** -0.5
LOG2E = 1.4426950408889634

LANES = 128
HEADS_PER_SLAB = LANES // HEAD_DIM
VMEM_LIMIT_BYTES = 56 * 1024 * 1024

TOKEN_TILE = 512
FFN_TILE = 1024
FF_CHUNKS = 11
SB_TILE = 64
SB_BAND = 256
SB_CHUNK = 128
SB_INTERLEAVE = 8
F32_EXP_UNDERFLOW = -104.0


def _rms(x, g):
    return x * lax.rsqrt(jnp.mean(x * x, axis=-1, keepdims=True) + RMS_EPS) * g


def _params(semantics):
    return pltpu.CompilerParams(dimension_semantics=semantics, vmem_limit_bytes=VMEM_LIMIT_BYTES)


def _resident(shape):
    return pl.BlockSpec(shape, lambda *_: (0,) * len(shape), pipeline_mode=pl.Buffered(1))


def _ffn_kernel(*refs, has_attn, has_final_norm):
    refs = list(refs)
    h_ref = refs.pop(0)
    if has_attn:
        o_ref, wo_ref = refs.pop(0), refs.pop(0)
    g_ref, win_ref, wout_ref = refs.pop(0), refs.pop(0), refs.pop(0)
    if has_final_norm:
        fg_ref = refs.pop(0)
    (out_ref,) = refs

    x = h_ref[...]
    if has_attn:
        x = x + jnp.dot(o_ref[...], wo_ref[...], preferred_element_type=F32)
    xn = _rms(x, g_ref[...]).astype(BF16)
    fc = D_FF // FF_CHUNKS
    y = None
    for c in range(FF_CHUNKS):
        gate = jnp.dot(xn, win_ref[:, c * fc:(c + 1) * fc], preferred_element_type=F32)
        up = jnp.dot(xn, win_ref[:, D_FF + c * fc:D_FF + (c + 1) * fc], preferred_element_type=F32)
        act = (gate * jax.nn.sigmoid(gate) * up).astype(BF16)
        yc = jnp.dot(act, wout_ref[c * fc:(c + 1) * fc, :], preferred_element_type=F32)
        y = yc if y is None else y + yc
    x = x + FFN_RES_SCALE * y
    if has_final_norm:
        x = _rms(x, fg_ref[...])
    out_ref[...] = x


def _ffn(h, g, w_in, w_out, attn=None, final_g=None, name="ffn"):
    n, d = h.shape
    row = pl.BlockSpec((FFN_TILE, d), lambda i: (i, 0))
    args, specs = [h], [row]
    if attn is not None:
        o, w_o = attn
        args += [o, w_o]
        specs += [row, _resident(w_o.shape)]
    args += [g.reshape(1, d), w_in, w_out]
    specs += [_resident((1, d)), _resident(w_in.shape), _resident(w_out.shape)]
    if final_g is not None:
        args.append(final_g.reshape(1, d))
        specs.append(_resident((1, d)))
    return pl.pallas_call(
        functools.partial(_ffn_kernel, has_attn=attn is not None, has_final_norm=final_g is not None),
        out_shape=jax.ShapeDtypeStruct((n, d), F32),
        grid=(n // FFN_TILE,),
        in_specs=specs,
        out_specs=row,
        compiler_params=_params(("parallel",)),
        name=name,
    )(*args)


def _proj_kernel(h_ref, g_ref, w_ref, cos_ref, sin_ref, out_ref, *, rot_slabs, dup_heads):
    xn = _rms(h_ref[...], g_ref[...]).astype(BF16)
    y = jnp.dot(xn, w_ref[...], preferred_element_type=F32)
    cos, sin = cos_ref[...], sin_ref[...]
    lane = lax.broadcasted_iota(jnp.int32, cos.shape, 1)
    first_half = (lane % HEAD_DIM) < (HEAD_DIM // 2)
    head0 = lane < HEAD_DIM
    for s in range(y.shape[1] // LANES):
        ys = y[:, s * LANES:(s + 1) * LANES]
        if s < rot_slabs:
            partner = jnp.where(first_half,
                                pltpu.roll(ys, LANES - HEAD_DIM // 2, 1),
                                pltpu.roll(ys, HEAD_DIM // 2, 1))
            ys = ys * cos + partner * sin
        if dup_heads:
            swapped = pltpu.roll(ys, HEAD_DIM, 1)
            out_ref[:, 2 * s * LANES:(2 * s + 1) * LANES] = jnp.where(head0, ys, swapped).astype(out_ref.dtype)
            out_ref[:, (2 * s + 1) * LANES:(2 * s + 2) * LANES] = jnp.where(head0, swapped, ys).astype(out_ref.dtype)
        else:
            out_ref[:, s * LANES:(s + 1) * LANES] = ys.astype(out_ref.dtype)


def _proj(h, g, w, cos, sin, rot_slabs, name, dup_heads=False):
    n, d = h.shape
    m = w.shape[1] * (HEADS_PER_SLAB if dup_heads else 1)
    tiles_per_seq = cos.shape[0] // TOKEN_TILE
    table = pl.BlockSpec((TOKEN_TILE, LANES), lambda i: (i % tiles_per_seq, 0))
    return pl.pallas_call(
        functools.partial(_proj_kernel, rot_slabs=rot_slabs, dup_heads=dup_heads),
        out_shape=jax.ShapeDtypeStruct((n, m), BF16),
        grid=(n // TOKEN_TILE,),
        in_specs=[pl.BlockSpec((TOKEN_TILE, d), lambda i: (i, 0)), _resident((1, d)), _resident(w.shape),
                  table, table],
        out_specs=pl.BlockSpec((TOKEN_TILE, m), lambda i: (i, 0)),
        compiler_params=_params(("parallel",)),
        name=name,
    )(h, g.reshape(1, d), w, cos, sin)


def _rotary_tables(seq):
    half = HEAD_DIM // 2
    inv_freq = ROPE_THETA ** (-jnp.arange(half, dtype=F32) / half)
    ang = jnp.arange(seq, dtype=F32)[:, None] * inv_freq[None, :]
    cos, sin = jnp.cos(ang), jnp.sin(ang)
    cos_head = jnp.concatenate([cos, cos], axis=-1)
    sin_head = jnp.concatenate([-sin, sin], axis=-1)
    return jnp.tile(cos_head, (1, HEADS_PER_SLAB)), jnp.tile(sin_head, (1, HEADS_PER_SLAB))


def _sb_kernel(q_ref, k_ref, v_ref, tri_ref, o_ref,
               mask_ref, lb_ref, split_ref, acc_ref, carry_ref, gqs_ref, gacc_ref, gcarry_ref):
    t, band, ck, u_tiles = SB_TILE, SB_BAND, SB_CHUNK, SB_INTERLEAVE
    nck = band // ck
    seq = q_ref.shape[0]
    q_tiles = seq // t
    head_tiles = band // t
    band_tiles = q_tiles - head_tiles
    sizes = [u_tiles] * (band_tiles // u_tiles) + ([band_tiles % u_tiles] if band_tiles % u_tiles else [])
    lane = lax.broadcasted_iota(jnp.int32, (t, LANES), 1)
    head0 = lane < HEAD_DIM
    qrow = lax.broadcasted_iota(jnp.int32, (2 * t, ck), 0) & (t - 1)
    col = lax.broadcasted_iota(jnp.int32, (2 * t, ck), 1)
    mask_ref[...] = jnp.where(col < qrow + (band - t - (nck - 1) * ck), 1.0, 0.0)

    def aligned(offset):
        return offset if isinstance(offset, int) else pl.multiple_of(offset, t)

    def stacked_queries(i):
        q = q_ref[pl.ds(aligned(i * t), t), :] * ATTN_SCALE
        zero = jnp.zeros_like(q)
        return jnp.concatenate([jnp.where(head0, q, zero), jnp.where(head0, zero, q)], axis=0)

    def key_rows(start):
        return pl.ds(aligned(start), band)

    def logits(qs, start):
        z = lax.dot_general(qs, k_ref[key_rows(start), :], (((1,), (1,)), ((), ())),
                            preferred_element_type=F32)
        sp = jnp.log(1.0 + jnp.exp2(jnp.abs(z) * -LOG2E))
        log_beta = jnp.minimum(z, 0.0) - sp
        return log_beta, log_beta - z

    def split_chunk(l):
        hi = l.astype(BF16)
        lo = (l - hi.astype(F32)).astype(BF16)
        return jnp.concatenate([hi, lo], axis=1)

    def suffix_sums(split):
        return jnp.dot(split, tri_ref[...], preferred_element_type=F32)

    def chunk(x, c):
        return x[:, c * ck:(c + 1) * ck]

    def write_output(i, acc):
        o_ref[pl.ds(aligned(i * t), t), :] = jnp.where(head0, acc[:t], acc[t:]).astype(o_ref.dtype)

    def band_start(i):
        return (i - (head_tiles - 1)) * t

    def logits_stage(parity, slot, i):
        log_beta, log_1m = logits(stacked_queries(i), band_start(i))
        lb_ref[parity, slot] = log_beta
        for c in range(nck):
            l = chunk(log_1m, c)
            if c == nck - 1:
                l = l * mask_ref[...]
            split_ref[parity, slot, c * 2 * t:(c + 1) * 2 * t, :] = split_chunk(l)

    def weights_stage(parity, slot, i, sums):
        log_beta = lb_ref[parity, slot]
        later = None
        ws = [None] * nck
        for c in reversed(range(nck)):
            sums_c = sums[c * 2 * t:(c + 1) * 2 * t]
            x = chunk(log_beta, c) + chunk(sums_c, 0)
            if later is not None:
                x = x + later
            w = jnp.exp(x)
            if c == nck - 1:
                w = w * mask_ref[...]
            ws[c] = w.astype(BF16)
            later = chunk(sums_c, 1) if later is None else later + chunk(sums_c, 1)
        acc = jnp.dot(jnp.concatenate(ws, axis=1), v_ref[key_rows(band_start(i)), :],
                      preferred_element_type=F32)
        acc_ref[parity, slot] = acc
        carry_ref[parity, slot] = later
        write_output(i, acc)
        return later

    def generic_sweep(i, start, limit):
        log_beta, log_1m = logits(gqs_ref[...], start)
        bound = jnp.minimum(i * t + qrow, limit) - start
        valids = [col + c * ck < bound for c in range(nck)]
        sums = suffix_sums(jnp.concatenate(
            [split_chunk(jnp.where(valids[c], chunk(log_1m, c), 0.0)) for c in range(nck)], axis=0))
        later = gcarry_ref[...]
        ws = [None] * nck
        for c in reversed(range(nck)):
            sums_c = sums[c * 2 * t:(c + 1) * 2 * t]
            x = chunk(log_beta, c) + chunk(sums_c, 0) + later
            ws[c] = jnp.where(valids[c], jnp.exp(x), 0.0).astype(BF16)
            later = later + chunk(sums_c, 1)
        gacc_ref[...] += jnp.dot(jnp.concatenate(ws, axis=1), v_ref[key_rows(start), :],
                                 preferred_element_type=F32)
        gcarry_ref[...] = later
        return jnp.max(later)

    def head_tile(i, c):
        gqs_ref[...] = stacked_queries(i)
        gacc_ref[...] = jnp.zeros_like(gacc_ref)
        gcarry_ref[...] = jnp.zeros_like(gcarry_ref)
        generic_sweep(i, 0, seq)
        write_output(i, gacc_ref[...])
        return c

    def continue_tile(parity, slot, i):
        gqs_ref[...] = stacked_queries(i)
        gacc_ref[...] = acc_ref[parity, slot]
        gcarry_ref[...] = carry_ref[parity, slot]

        def cond(state):
            limit, top = state
            return jnp.logical_and(limit > 0, top > F32_EXP_UNDERFLOW)

        def body(state):
            limit, _ = state
            start = jnp.maximum(limit - band, 0)
            return start, generic_sweep(i, start, limit)

        lax.while_loop(cond, body, (band_start(i), jnp.max(gcarry_ref[...])))
        write_output(i, gacc_ref[...])

    def first_tile(g):
        return head_tiles + (sum(sizes[:g]) if isinstance(g, int) else g * u_tiles)

    def stage(parity, group, size, next_group, next_size):
        sums = [suffix_sums(split_ref[parity, slot]) for slot in range(size)]
        if next_group is not None:
            for slot in range(next_size):
                logits_stage(1 - parity, slot, first_tile(next_group) + slot)
        top = None
        for slot in range(size):
            carry = weights_stage(parity, slot, first_tile(group) + slot, sums[slot])
            top = carry if top is None else jnp.maximum(top, carry)

        @pl.when(jnp.max(top) > F32_EXP_UNDERFLOW)
        def _():
            def one(slot, c):
                continue_tile(parity, slot, first_tile(group) + slot)
                return c
            lax.fori_loop(0, size, one, 0)

    lax.fori_loop(0, head_tiles, head_tile, 0)
    for slot in range(sizes[0]):
        logits_stage(0, slot, first_tile(0) + slot)

    def stage_pair(m, c):
        stage(0, 2 * m, u_tiles, 2 * m + 1, u_tiles)
        stage(1, 2 * m + 1, u_tiles, 2 * m + 2, u_tiles)
        return c

    full_groups = band_tiles // u_tiles
    pairs = (full_groups - 1) // 2
    lax.fori_loop(0, pairs, stage_pair, 0)
    for g in range(2 * pairs, len(sizes)):
        has_next = g + 1 < len(sizes)
        stage(g % 2, g, sizes[g], g + 1 if has_next else None, sizes[g + 1] if has_next else 0)


def _suffix_matrix(n):
    row = lax.broadcasted_iota(jnp.int32, (n, n), 0)
    col = lax.broadcasted_iota(jnp.int32, (n, n), 1)
    half = jnp.concatenate([(row > col).astype(BF16), jnp.ones((n, n), BF16)], axis=1)
    return jnp.concatenate([half, half], axis=0)


def _sb_attention(qkv, batch, seq):
    n = qkv.shape[0]
    slabs = SB_HEADS // HEADS_PER_SLAB
    tri = _suffix_matrix(SB_CHUNK)
    rows, nck = 2 * SB_TILE, SB_BAND // SB_CHUNK
    assert seq % SB_TILE == 0 and seq >= SB_BAND and SB_CHUNK == LANES
    return pl.pallas_call(
        _sb_kernel,
        out_shape=jax.ShapeDtypeStruct((n, SB_HEADS * HEAD_DIM), BF16),
        grid=(batch, slabs),
        in_specs=[
            pl.BlockSpec((seq, LANES), lambda b, s: (b, s)),
            pl.BlockSpec((seq, LANES), lambda b, s: (b, slabs + s)),
            pl.BlockSpec((seq, LANES), lambda b, s: (b, 2 * slabs + s)),
            _resident(tri.shape),
        ],
        out_specs=pl.BlockSpec((seq, LANES), lambda b, s: (b, s)),
        scratch_shapes=[
            pltpu.VMEM((rows, SB_CHUNK), F32),
            pltpu.VMEM((2, SB_INTERLEAVE, rows, SB_BAND), F32),
            pltpu.VMEM((2, SB_INTERLEAVE, nck * rows, 2 * SB_CHUNK), BF16),
            pltpu.VMEM((2, SB_INTERLEAVE, rows, LANES), F32),
            pltpu.VMEM((2, SB_INTERLEAVE, rows, LANES), F32),
            pltpu.VMEM((rows, LANES), BF16),
            pltpu.VMEM((rows, LANES), F32),
            pltpu.VMEM((rows, LANES), F32),
        ],
        compiler_params=_params(("parallel", "parallel")),
        name="sb_attention",
    )(qkv, qkv, qkv, tri)


def _swa_kernel(sink_ref, bias_ref, q_ref, kvp_ref, kvc_ref, o_ref):
    w = WINDOW
    bias = jnp.concatenate([bias_ref[0]] * SWA_GROUP, axis=0)
    lane = lax.broadcasted_iota(jnp.int32, (w, LANES), 1)
    head0 = lane < HEAD_DIM
    zero = jnp.zeros((w, LANES), BF16)
    ones = jnp.ones((2 * w, LANES), BF16)
    for h in range(SWA_KV_HEADS):
        k_cols = slice(h * LANES, (h + 1) * LANES)
        v_cols = slice((SWA_KV_HEADS + h) * LANES, (SWA_KV_HEADS + h + 1) * LANES)
        kk = jnp.concatenate([kvp_ref[:, k_cols], kvc_ref[:, k_cols]], axis=0)
        vv = jnp.concatenate([kvp_ref[:, v_cols], kvc_ref[:, v_cols]], axis=0)
        vv_ones = jnp.concatenate([vv, ones], axis=1)
        stacked, sinks = [], []
        for g in range(SWA_GROUP):
            head = h * SWA_GROUP + g
            slab = head // HEADS_PER_SLAB
            q = q_ref[:, slab * LANES:(slab + 1) * LANES] * ATTN_SCALE
            stacked.append(jnp.where(head0, q, zero) if head % HEADS_PER_SLAB == 0 else jnp.where(head0, zero, q))
            sinks.append(jnp.full((w, LANES), sink_ref[head], F32))
        qs = jnp.concatenate(stacked, axis=0)
        sink = jnp.concatenate(sinks, axis=0)
        s = lax.dot_general(qs, kk, (((1,), (1,)), ((), ())), preferred_element_type=F32) + bias
        m = jnp.maximum(jnp.broadcast_to(jnp.max(s, axis=1, keepdims=True), sink.shape), sink)
        e = jnp.exp(s - jnp.concatenate([m, m], axis=1)).astype(BF16)
        ov = jnp.dot(e, vv_ones, preferred_element_type=F32)
        denom = ov[:, LANES:] + jnp.exp(sink - m)
        o = ov[:, :LANES] * (1.0 / denom)
        for pair in range(SWA_GROUP // HEADS_PER_SLAB):
            slab = (h * SWA_GROUP) // HEADS_PER_SLAB + pair
            lo = o[(2 * pair) * w:(2 * pair + 1) * w]
            hi = o[(2 * pair + 1) * w:(2 * pair + 2) * w]
            o_ref[:, slab * LANES:(slab + 1) * LANES] = jnp.where(head0, lo, hi).astype(o_ref.dtype)


def _swa_attention(q, kv, sinks, batch, seq):
    n, dq = q.shape
    nb = seq // WINDOW
    dkv = kv.shape[1]
    qi = lax.broadcasted_iota(jnp.int32, (WINDOW, 2 * WINDOW), 0)
    ki = lax.broadcasted_iota(jnp.int32, (WINDOW, 2 * WINDOW), 1)
    diff = qi + WINDOW - ki
    in_window = (diff >= 0) & (diff < WINDOW)
    bias = jnp.where(jnp.stack([in_window & (ki >= WINDOW), in_window]), 0.0, -jnp.inf).astype(F32)
    return pl.pallas_call(
        _swa_kernel,
        out_shape=jax.ShapeDtypeStruct((n, dq), BF16),
        grid=(batch, nb),
        in_specs=[
            pl.BlockSpec(memory_space=pltpu.SMEM),
            pl.BlockSpec((1, WINDOW, 2 * WINDOW), lambda b, i: (jnp.minimum(i, 1), 0, 0)),
            pl.BlockSpec((WINDOW, dq), lambda b, i: (b * nb + i, 0)),
            pl.BlockSpec((WINDOW, dkv), lambda b, i: (b * nb + jnp.maximum(i - 1, 0), 0)),
            pl.BlockSpec((WINDOW, dkv), lambda b, i: (b * nb + i, 0)),
        ],
        out_specs=pl.BlockSpec((WINDOW, dq), lambda b, i: (b * nb + i, 0)),
        compiler_params=_params(("parallel", "arbitrary")),
        name="swa_attention",
    )(sinks, bias, q, kv, kv)


def kernel(x, ffn1_norm, ffn1_w_in, ffn1_w_out, mix_norm, ffn2_norm, ffn2_w_in, ffn2_w_out,
           sb_w_qkv, sb_w_o, kv_norm, kv_w, swa_w_q, swa_sinks, swa_w_o, final_norm):
    batch, seq, d = x.shape
    n = batch * seq
    cos, sin = _rotary_tables(seq)
    bf = lambda a: a.astype(BF16)
    h = x.reshape(n, d)

    h = _ffn(h, ffn1_norm[0], bf(ffn1_w_in[0]), bf(ffn1_w_out[0]), name="ffn1_l0")
    qkv = _proj(h, mix_norm[0], bf(sb_w_qkv[0]), cos, sin, rot_slabs=0, name="proj_qkv")
    o = _sb_attention(qkv, batch, seq)
    h = _ffn(h, ffn2_norm[0], bf(ffn2_w_in[0]), bf(ffn2_w_out[0]), attn=(o, bf(sb_w_o[0])), name="ffn2_l0")
    k_slabs = SWA_KV_HEADS * HEAD_DIM // LANES
    kv = _proj(h, kv_norm, bf(kv_w), cos, sin, rot_slabs=k_slabs, name="proj_kv", dup_heads=True)

    h = _ffn(h, ffn1_norm[1], bf(ffn1_w_in[1]), bf(ffn1_w_out[1]), name="ffn1_l1")
    q = _proj(h, mix_norm[1], bf(swa_w_q[0]), cos, sin, rot_slabs=SWA_Q_HEADS * HEAD_DIM // LANES,
              name="proj_q")
    o = _swa_attention(q, kv, swa_sinks[0], batch, seq)
    h = _ffn(h, ffn2_norm[1], bf(ffn2_w_in[1]), bf(ffn2_w_out[1]), attn=(o, bf(swa_w_o[0])),
             final_g=final_norm, name="ffn2_l1")
    return h.reshape(batch, seq, d)
```

```python
import functools

import jax
import jax.numpy as jnp
from jax import lax
from jax.experimental import pallas as pl
from jax.experimental.pallas import tpu as pltpu

F32 = jnp.float32
BF16 = jnp.bfloat16

D_MODEL = 1024
HEAD_DIM = 64
SB_HEADS = 16
SWA_Q_HEADS = 16
SWA_KV_HEADS = 4
SWA_GROUP = SWA_Q_HEADS // SWA_KV_HEADS
WINDOW = 128
D_FF = 2816
ROPE_THETA = 10000.0
RMS_EPS = 1e-6
FFN_RES_SCALE = 0.5
ATTN_SCALE = HEAD_DIM ** -0.5
LOG2E = 1.4426950408889634

LANES = 128
HEADS_PER_SLAB = LANES // HEAD_DIM
VMEM_LIMIT_BYTES = 56 * 1024 * 1024

TOKEN_TILE = 1024
PROJ_CHUNK = 256
CAST_BLOCK_BYTES = 4 * 1024 * 1024
FFN_TILE = 1024
FF_CHUNKS = 11
SB_TILE = 64
SB_BAND = 256
SB_CHUNK = 128
SB_INTERLEAVE = 8
F32_EXP_UNDERFLOW = -104.0


def _rms(x, g):
    return x * lax.rsqrt(jnp.mean(x * x, axis=-1, keepdims=True) + RMS_EPS) * g


def _params(semantics):
    return pltpu.CompilerParams(dimension_semantics=semantics, vmem_limit_bytes=VMEM_LIMIT_BYTES)


def _resident(shape):
    return pl.BlockSpec(shape, lambda *_: (0,) * len(shape), pipeline_mode=pl.Buffered(1))


def _cast_kernel(w_ref, out_ref):
    out_ref[...] = w_ref[...].astype(out_ref.dtype)


def _to_bf16(w, layer=None, name="cast"):
    rows, cols = w.shape[-2:]
    block_rows = max(8, min(rows, CAST_BLOCK_BYTES // (4 * cols) // 8 * 8))
    while rows % block_rows:
        block_rows -= 8
    if layer is None:
        in_spec = pl.BlockSpec((block_rows, cols), lambda i: (i, 0))
    else:
        in_spec = pl.BlockSpec((None, block_rows, cols), lambda i: (layer, i, 0))
    return pl.pallas_call(
        _cast_kernel,
        out_shape=jax.ShapeDtypeStruct((rows, cols), BF16),
        grid=(rows // block_rows,),
        in_specs=[in_spec],
        out_specs=pl.BlockSpec((block_rows, cols), lambda i: (i, 0)),
        compiler_params=_params(("parallel",)),
        name=name,
    )(w)


def _ffn_kernel(*refs, has_attn, has_final_norm):
    refs = list(refs)
    h_ref = refs.pop(0)
    if has_attn:
        o_ref, wo_ref = refs.pop(0), refs.pop(0)
    g_ref, win_ref, wout_ref = refs.pop(0), refs.pop(0), refs.pop(0)
    if has_final_norm:
        fg_ref = refs.pop(0)
    (out_ref,) = refs

    x = h_ref[...]
    if has_attn:
        x = x + jnp.dot(o_ref[...], wo_ref[...], preferred_element_type=F32)
    xn = _rms(x, g_ref[...]).astype(BF16)
    fc = D_FF // FF_CHUNKS
    y = None
    for c in range(FF_CHUNKS):
        gate = jnp.dot(xn, win_ref[:, c * fc:(c + 1) * fc], preferred_element_type=F32)
        up = jnp.dot(xn, win_ref[:, D_FF + c * fc:D_FF + (c + 1) * fc], preferred_element_type=F32)
        act = (gate * jax.nn.sigmoid(gate) * up).astype(BF16)
        yc = jnp.dot(act, wout_ref[c * fc:(c + 1) * fc, :], preferred_element_type=F32)
        y = yc if y is None else y + yc
    x = x + FFN_RES_SCALE * y
    if has_final_norm:
        x = _rms(x, fg_ref[...])
    out_ref[...] = x


def _ffn(h, g, w_in, w_out, attn=None, final_g=None, name="ffn"):
    n, d = h.shape
    row = pl.BlockSpec((FFN_TILE, d), lambda i: (i, 0))
    args, specs = [h], [row]
    if attn is not None:
        o, w_o = attn
        args += [o, w_o]
        specs += [row, _resident(w_o.shape)]
    args += [g.reshape(1, d), w_in, w_out]
    specs += [_resident((1, d)), _resident(w_in.shape), _resident(w_out.shape)]
    if final_g is not None:
        args.append(final_g.reshape(1, d))
        specs.append(_resident((1, d)))
    return pl.pallas_call(
        functools.partial(_ffn_kernel, has_attn=attn is not None, has_final_norm=final_g is not None),
        out_shape=jax.ShapeDtypeStruct((n, d), F32),
        grid=(n // FFN_TILE,),
        in_specs=specs,
        out_specs=row,
        compiler_params=_params(("parallel",)),
        name=name,
    )(*args)


def _proj_kernel(h_ref, g_ref, w_ref, cos_ref, sin_ref, out_ref, *, rot_slabs, dup_heads):
    xn = _rms(h_ref[...], g_ref[...]).astype(BF16)
    cos, sin = cos_ref[...], sin_ref[...]
    lane = lax.broadcasted_iota(jnp.int32, cos.shape, 1)
    first_half = (lane % HEAD_DIM) < (HEAD_DIM // 2)
    head0 = lane < HEAD_DIM
    slabs_per_chunk = PROJ_CHUNK // LANES
    for s in range(w_ref.shape[1] // LANES):
        if s % slabs_per_chunk == 0:
            y = jnp.dot(xn, w_ref[:, s * LANES:(s + slabs_per_chunk) * LANES], preferred_element_type=F32)
        ys = y[:, (s % slabs_per_chunk) * LANES:(s % slabs_per_chunk + 1) * LANES]
        if s < rot_slabs:
            partner = jnp.where(first_half,
                                pltpu.roll(ys, LANES - HEAD_DIM // 2, 1),
                                pltpu.roll(ys, HEAD_DIM // 2, 1))
            ys = ys * cos + partner * sin
        if dup_heads:
            swapped = pltpu.roll(ys, HEAD_DIM, 1)
            out_ref[:, 2 * s * LANES:(2 * s + 1) * LANES] = jnp.where(head0, ys, swapped).astype(out_ref.dtype)
            out_ref[:, (2 * s + 1) * LANES:(2 * s + 2) * LANES] = jnp.where(head0, swapped, ys).astype(out_ref.dtype)
        else:
            out_ref[:, s * LANES:(s + 1) * LANES] = ys.astype(out_ref.dtype)


def _proj(h, g, w, cos, sin, rot_slabs, name, dup_heads=False):
    n, d = h.shape
    m = w.shape[1] * (HEADS_PER_SLAB if dup_heads else 1)
    tiles_per_seq = cos.shape[0] // TOKEN_TILE
    table = pl.BlockSpec((TOKEN_TILE, LANES), lambda i: (i % tiles_per_seq, 0))
    return pl.pallas_call(
        functools.partial(_proj_kernel, rot_slabs=rot_slabs, dup_heads=dup_heads),
        out_shape=jax.ShapeDtypeStruct((n, m), BF16),
        grid=(n // TOKEN_TILE,),
        in_specs=[pl.BlockSpec((TOKEN_TILE, d), lambda i: (i, 0)), _resident((1, d)), _resident(w.shape),
                  table, table],
        out_specs=pl.BlockSpec((TOKEN_TILE, m), lambda i: (i, 0)),
        compiler_params=_params(("parallel",)),
        name=name,
    )(h, g.reshape(1, d), w, cos, sin)


def _rotary_tables(seq):
    half = HEAD_DIM // 2
    inv_freq = ROPE_THETA ** (-jnp.arange(half, dtype=F32) / half)
    ang = jnp.arange(seq, dtype=F32)[:, None] * inv_freq[None, :]
    cos, sin = jnp.cos(ang), jnp.sin(ang)
    cos_head = jnp.concatenate([cos, cos], axis=-1)
    sin_head = jnp.concatenate([-sin, sin], axis=-1)
    return jnp.tile(cos_head, (1, HEADS_PER_SLAB)), jnp.tile(sin_head, (1, HEADS_PER_SLAB))


def _sb_kernel(q_ref, k_ref, v_ref, tri_ref, o_ref,
               mask_ref, lb_ref, split_ref, acc_ref, carry_ref, gqs_ref, gacc_ref, gcarry_ref):
    t, band, ck, u_tiles = SB_TILE, SB_BAND, SB_CHUNK, SB_INTERLEAVE
    nck = band // ck
    seq = q_ref.shape[0]
    q_tiles = seq // t
    head_tiles = band // t
    band_tiles = q_tiles - head_tiles
    sizes = [u_tiles] * (band_tiles // u_tiles) + ([band_tiles % u_tiles] if band_tiles % u_tiles else [])
    lane = lax.broadcasted_iota(jnp.int32, (t, LANES), 1)
    head0 = lane < HEAD_DIM
    qrow = lax.broadcasted_iota(jnp.int32, (2 * t, ck), 0) & (t - 1)
    col = lax.broadcasted_iota(jnp.int32, (2 * t, ck), 1)
    mask_ref[...] = jnp.where(col < qrow + (band - t - (nck - 1) * ck), 1.0, 0.0)

    def aligned(offset):
        return offset if isinstance(offset, int) else pl.multiple_of(offset, t)

    def stacked_queries(i):
        q = q_ref[pl.ds(aligned(i * t), t), :] * ATTN_SCALE
        zero = jnp.zeros_like(q)
        return jnp.concatenate([jnp.where(head0, q, zero), jnp.where(head0, zero, q)], axis=0)

    def key_rows(start):
        return pl.ds(aligned(start), band)

    def logits(qs, start):
        z = lax.dot_general(qs, k_ref[key_rows(start), :], (((1,), (1,)), ((), ())),
                            preferred_element_type=F32)
        sp = jnp.log(1.0 + jnp.exp2(jnp.abs(z) * -LOG2E))
        log_beta = jnp.minimum(z, 0.0) - sp
        return log_beta, log_beta - z

    def split_chunk(l):
        hi = l.astype(BF16)
        lo = (l - hi.astype(F32)).astype(BF16)
        return jnp.concatenate([hi, lo], axis=1)

    def suffix_sums(split):
        return jnp.dot(split, tri_ref[...], preferred_element_type=F32)

    def chunk(x, c):
        return x[:, c * ck:(c + 1) * ck]

    def write_output(i, acc):
        o_ref[pl.ds(aligned(i * t), t), :] = jnp.where(head0, acc[:t], acc[t:]).astype(o_ref.dtype)

    def band_start(i):
        return (i - (head_tiles - 1)) * t

    def logits_stage(parity, slot, i):
        log_beta, log_1m = logits(stacked_queries(i), band_start(i))
        lb_ref[parity, slot] = log_beta
        for c in range(nck):
            l = chunk(log_1m, c)
            if c == nck - 1:
                l = l * mask_ref[...]
            split_ref[parity, slot, c * 2 * t:(c + 1) * 2 * t, :] = split_chunk(l)

    def weights_stage(parity, slot, i, sums):
        log_beta = lb_ref[parity, slot]
        later = None
        ws = [None] * nck
        for c in reversed(range(nck)):
            sums_c = sums[c * 2 * t:(c + 1) * 2 * t]
            x = chunk(log_beta, c) + chunk(sums_c, 0)
            if later is not None:
                x = x + later
            w = jnp.exp(x)
            if c == nck - 1:
                w = w * mask_ref[...]
            ws[c] = w.astype(BF16)
            later = chunk(sums_c, 1) if later is None else later + chunk(sums_c, 1)
        acc = jnp.dot(jnp.concatenate(ws, axis=1), v_ref[key_rows(band_start(i)), :],
                      preferred_element_type=F32)
        acc_ref[parity, slot] = acc
        carry_ref[parity, slot] = later
        write_output(i, acc)
        return later

    def generic_sweep(i, start, limit):
        log_beta, log_1m = logits(gqs_ref[...], start)
        bound = jnp.minimum(i * t + qrow, limit) - start
        valids = [col + c * ck < bound for c in range(nck)]
        sums = suffix_sums(jnp.concatenate(
            [split_chunk(jnp.where(valids[c], chunk(log_1m, c), 0.0)) for c in range(nck)], axis=0))
        later = gcarry_ref[...]
        ws = [None] * nck
        for c in reversed(range(nck)):
            sums_c = sums[c * 2 * t:(c + 1) * 2 * t]
            x = chunk(log_beta, c) + chunk(sums_c, 0) + later
            ws[c] = jnp.where(valids[c], jnp.exp(x), 0.0).astype(BF16)
            later = later + chunk(sums_c, 1)
        gacc_ref[...] += jnp.dot(jnp.concatenate(ws, axis=1), v_ref[key_rows(start), :],
                                 preferred_element_type=F32)
        gcarry_ref[...] = later
        return jnp.max(later)

    def head_tile(i, c):
        gqs_ref[...] = stacked_queries(i)
        gacc_ref[...] = jnp.zeros_like(gacc_ref)
        gcarry_ref[...] = jnp.zeros_like(gcarry_ref)
        generic_sweep(i, 0, seq)
        write_output(i, gacc_ref[...])
        return c

    def continue_tile(parity, slot, i):
        gqs_ref[...] = stacked_queries(i)
        gacc_ref[...] = acc_ref[parity, slot]
        gcarry_ref[...] = carry_ref[parity, slot]

        def cond(state):
            limit, top = state
            return jnp.logical_and(limit > 0, top > F32_EXP_UNDERFLOW)

        def body(state):
            limit, _ = state
            start = jnp.maximum(limit - band, 0)
            return start, generic_sweep(i, start, limit)

        lax.while_loop(cond, body, (band_start(i), jnp.max(gcarry_ref[...])))
        write_output(i, gacc_ref[...])

    def first_tile(g):
        return head_tiles + (sum(sizes[:g]) if isinstance(g, int) else g * u_tiles)

    def stage(parity, group, size, next_group, next_size):
        sums = [suffix_sums(split_ref[parity, slot]) for slot in range(size)]
        if next_group is not None:
            for slot in range(next_size):
                logits_stage(1 - parity, slot, first_tile(next_group) + slot)
        top = None
        for slot in range(size):
            carry = weights_stage(parity, slot, first_tile(group) + slot, sums[slot])
            top = carry if top is None else jnp.maximum(top, carry)

        @pl.when(jnp.max(top) > F32_EXP_UNDERFLOW)
        def _():
            def one(slot, c):
                continue_tile(parity, slot, first_tile(group) + slot)
                return c
            lax.fori_loop(0, size, one, 0)

    lax.fori_loop(0, head_tiles, head_tile, 0)
    for slot in range(sizes[0]):
        logits_stage(0, slot, first_tile(0) + slot)

    def stage_pair(m, c):
        stage(0, 2 * m, u_tiles, 2 * m + 1, u_tiles)
        stage(1, 2 * m + 1, u_tiles, 2 * m + 2, u_tiles)
        return c

    full_groups = band_tiles // u_tiles
    pairs = (full_groups - 1) // 2
    lax.fori_loop(0, pairs, stage_pair, 0)
    for g in range(2 * pairs, len(sizes)):
        has_next = g + 1 < len(sizes)
        stage(g % 2, g, sizes[g], g + 1 if has_next else None, sizes[g + 1] if has_next else 0)


def _suffix_matrix(n):
    row = lax.broadcasted_iota(jnp.int32, (n, n), 0)
    col = lax.broadcasted_iota(jnp.int32, (n, n), 1)
    half = jnp.concatenate([(row > col).astype(BF16), jnp.ones((n, n), BF16)], axis=1)
    return jnp.concatenate([half, half], axis=0)


def _sb_attention(qkv, batch, seq):
    n = qkv.shape[0]
    slabs = SB_HEADS // HEADS_PER_SLAB
    tri = _suffix_matrix(SB_CHUNK)
    rows, nck = 2 * SB_TILE, SB_BAND // SB_CHUNK
    assert seq % SB_TILE == 0 and seq >= SB_BAND and SB_CHUNK == LANES
    return pl.pallas_call(
        _sb_kernel,
        out_shape=jax.ShapeDtypeStruct((n, SB_HEADS * HEAD_DIM), BF16),
        grid=(batch, slabs),
        in_specs=[
            pl.BlockSpec((seq, LANES), lambda b, s: (b, s)),
            pl.BlockSpec((seq, LANES), lambda b, s: (b, slabs + s)),
            pl.BlockSpec((seq, LANES), lambda b, s: (b, 2 * slabs + s)),
            _resident(tri.shape),
        ],
        out_specs=pl.BlockSpec((seq, LANES), lambda b, s: (b, s)),
        scratch_shapes=[
            pltpu.VMEM((rows, SB_CHUNK), F32),
            pltpu.VMEM((2, SB_INTERLEAVE, rows, SB_BAND), F32),
            pltpu.VMEM((2, SB_INTERLEAVE, nck * rows, 2 * SB_CHUNK), BF16),
            pltpu.VMEM((2, SB_INTERLEAVE, rows, LANES), F32),
            pltpu.VMEM((2, SB_INTERLEAVE, rows, LANES), F32),
            pltpu.VMEM((rows, LANES), BF16),
            pltpu.VMEM((rows, LANES), F32),
            pltpu.VMEM((rows, LANES), F32),
        ],
        compiler_params=_params(("parallel", "parallel")),
        name="sb_attention",
    )(qkv, qkv, qkv, tri)


def _swa_kernel(sink_ref, bias_ref, q_ref, kvp_ref, kvc_ref, o_ref):
    w = WINDOW
    bias = jnp.concatenate([bias_ref[0]] * SWA_GROUP, axis=0)
    lane = lax.broadcasted_iota(jnp.int32, (w, LANES), 1)
    head0 = lane < HEAD_DIM
    zero = jnp.zeros((w, LANES), BF16)
    ones = jnp.ones((2 * w, LANES), BF16)
    for h in range(SWA_KV_HEADS):
        k_cols = slice(h * LANES, (h + 1) * LANES)
        v_cols = slice((SWA_KV_HEADS + h) * LANES, (SWA_KV_HEADS + h + 1) * LANES)
        kk = jnp.concatenate([kvp_ref[:, k_cols], kvc_ref[:, k_cols]], axis=0)
        vv = jnp.concatenate([kvp_ref[:, v_cols], kvc_ref[:, v_cols]], axis=0)
        vv_ones = jnp.concatenate([vv, ones], axis=1)
        stacked, sinks = [], []
        for g in range(SWA_GROUP):
            head = h * SWA_GROUP + g
            slab = head // HEADS_PER_SLAB
            q = q_ref[:, slab * LANES:(slab + 1) * LANES] * ATTN_SCALE
            stacked.append(jnp.where(head0, q, zero) if head % HEADS_PER_SLAB == 0 else jnp.where(head0, zero, q))
            sinks.append(jnp.full((w, LANES), sink_ref[head], F32))
        qs = jnp.concatenate(stacked, axis=0)
        sink = jnp.concatenate(sinks, axis=0)
        s = lax.dot_general(qs, kk, (((1,), (1,)), ((), ())), preferred_element_type=F32) + bias
        m = jnp.maximum(jnp.broadcast_to(jnp.max(s, axis=1, keepdims=True), sink.shape), sink)
        e = jnp.exp(s - jnp.concatenate([m, m], axis=1)).astype(BF16)
        ov = jnp.dot(e, vv_ones, preferred_element_type=F32)
        denom = ov[:, LANES:] + jnp.exp(sink - m)
        o = ov[:, :LANES] * (1.0 / denom)
        for pair in range(SWA_GROUP // HEADS_PER_SLAB):
            slab = (h * SWA_GROUP) // HEADS_PER_SLAB + pair
            lo = o[(2 * pair) * w:(2 * pair + 1) * w]
            hi = o[(2 * pair + 1) * w:(2 * pair + 2) * w]
            o_ref[:, slab * LANES:(slab + 1) * LANES] = jnp.where(head0, lo, hi).astype(o_ref.dtype)


def _swa_attention(q, kv, sinks, batch, seq):
    n, dq = q.shape
    nb = seq // WINDOW
    dkv = kv.shape[1]
    qi = lax.broadcasted_iota(jnp.int32, (WINDOW, 2 * WINDOW), 0)
    ki = lax.broadcasted_iota(jnp.int32, (WINDOW, 2 * WINDOW), 1)
    diff = qi + WINDOW - ki
    in_window = (diff >= 0) & (diff < WINDOW)
    bias = jnp.where(jnp.stack([in_window & (ki >= WINDOW), in_window]), 0.0, -jnp.inf).astype(F32)
    return pl.pallas_call(
        _swa_kernel,
        out_shape=jax.ShapeDtypeStruct((n, dq), BF16),
        grid=(batch, nb),
        in_specs=[
            pl.BlockSpec(memory_space=pltpu.SMEM),
            pl.BlockSpec((1, WINDOW, 2 * WINDOW), lambda b, i: (jnp.minimum(i, 1), 0, 0)),
            pl.BlockSpec((WINDOW, dq), lambda b, i: (b * nb + i, 0)),
            pl.BlockSpec((WINDOW, dkv), lambda b, i: (b * nb + jnp.maximum(i - 1, 0), 0)),
            pl.BlockSpec((WINDOW, dkv), lambda b, i: (b * nb + i, 0)),
        ],
        out_specs=pl.BlockSpec((WINDOW, dq), lambda b, i: (b * nb + i, 0)),
        compiler_params=_params(("parallel", "arbitrary")),
        name="swa_attention",
    )(sinks, bias, q, kv, kv)


def kernel(x, ffn1_norm, ffn1_w_in, ffn1_w_out, mix_norm, ffn2_norm, ffn2_w_in, ffn2_w_out,
           sb_w_qkv, sb_w_o, kv_norm, kv_w, swa_w_q, swa_sinks, swa_w_o, final_norm):
    batch, seq, d = x.shape
    n = batch * seq
    cos, sin = _rotary_tables(seq)
    bf = _to_bf16
    h = x.reshape(n, d)

    h = _ffn(h, ffn1_norm[0], bf(ffn1_w_in, 0), bf(ffn1_w_out, 0), name="ffn1_l0")
    qkv = _proj(h, mix_norm[0], bf(sb_w_qkv, 0), cos, sin, rot_slabs=0, name="proj_qkv")
    o = _sb_attention(qkv, batch, seq)
    h = _ffn(h, ffn2_norm[0], bf(ffn2_w_in, 0), bf(ffn2_w_out, 0), attn=(o, bf(sb_w_o, 0)), name="ffn2_l0")
    k_slabs = SWA_KV_HEADS * HEAD_DIM // LANES
    kv = _proj(h, kv_norm, bf(kv_w), cos, sin, rot_slabs=k_slabs, name="proj_kv", dup_heads=True)

    h = _ffn(h, ffn1_norm[1], bf(ffn1_w_in, 1), bf(ffn1_w_out, 1), name="ffn1_l1")
    q = _proj(h, mix_norm[1], bf(swa_w_q, 0), cos, sin, rot_slabs=SWA_Q_HEADS * HEAD_DIM // LANES,
              name="proj_q")
    o = _swa_attention(q, kv, swa_sinks[0], batch, seq)
    h = _ffn(h, ffn2_norm[1], bf(ffn2_w_in, 1), bf(ffn2_w_out, 1), attn=(o, bf(swa_w_o, 0)),
             final_g=final_norm, name="ffn2_l1")
    return h.reshape(batch, seq, d)
```

```python
import functools

import jax
import jax.numpy as jnp
from jax import lax
from jax.experimental import pallas as pl
from jax.experimental.pallas import tpu as pltpu

F32 = jnp.float32
BF16 = jnp.bfloat16

D_MODEL = 1024
HEAD_DIM = 64
SB_HEADS = 16
SWA_Q_HEADS = 16
SWA_KV_HEADS = 4
SWA_GROUP = SWA_Q_HEADS // SWA_KV_HEADS
WINDOW = 128
D_FF = 2816
ROPE_THETA = 10000.0
RMS_EPS = 1e-6
FFN_RES_SCALE = 0.5
ATTN_SCALE = HEAD_DIM ** -0.5
LOG2E = 1.4426950408889634

LANES = 128
HEADS_PER_SLAB = LANES // HEAD_DIM
VMEM_LIMIT_BYTES = 56 * 1024 * 1024

TOKEN_TILE = 1024
PROJ_CHUNK = 256
CAST_BLOCK_BYTES = 4 * 1024 * 1024
FFN_TILE = 1024
FF_CHUNKS = 11
SB_TILE = 64
SB_BAND = 256
SB_CHUNK = 128
SB_INTERLEAVE = 8
F32_EXP_UNDERFLOW = -104.0


def _rms(x, g):
    return x * lax.rsqrt(jnp.mean(x * x, axis=-1, keepdims=True) + RMS_EPS) * g


def _params(semantics):
    return pltpu.CompilerParams(dimension_semantics=semantics, vmem_limit_bytes=VMEM_LIMIT_BYTES)


def _resident(shape):
    return pl.BlockSpec(shape, lambda *_: (0,) * len(shape), pipeline_mode=pl.Buffered(1))


def _cast_kernel(w_ref, out_ref):
    out_ref[...] = w_ref[...].astype(out_ref.dtype)


def _to_bf16(w, layer=None, name="cast"):
    rows, cols = w.shape[-2:]
    block_rows = max(8, min(rows, CAST_BLOCK_BYTES // (4 * cols) // 8 * 8))
    while rows % block_rows:
        block_rows -= 8
    if layer is None:
        in_spec = pl.BlockSpec((block_rows, cols), lambda i: (i, 0))
    else:
        in_spec = pl.BlockSpec((None, block_rows, cols), lambda i: (layer, i, 0))
    return pl.pallas_call(
        _cast_kernel,
        out_shape=jax.ShapeDtypeStruct((rows, cols), BF16),
        grid=(rows // block_rows,),
        in_specs=[in_spec],
        out_specs=pl.BlockSpec((block_rows, cols), lambda i: (i, 0)),
        compiler_params=_params(("parallel",)),
        name=name,
    )(w)


def _ffn_kernel(*refs, has_attn, has_final_norm):
    refs = list(refs)
    h_ref = refs.pop(0)
    if has_attn:
        o_ref, wo_ref = refs.pop(0), refs.pop(0)
    g_ref, win_ref, wout_ref = refs.pop(0), refs.pop(0), refs.pop(0)
    if has_final_norm:
        fg_ref = refs.pop(0)
    (out_ref,) = refs

    x = h_ref[...]
    if has_attn:
        x = x + jnp.dot(o_ref[...], wo_ref[...], preferred_element_type=F32)
    xn = _rms(x, g_ref[...]).astype(BF16)
    fc = D_FF // FF_CHUNKS
    y = None
    for c in range(FF_CHUNKS):
        gate = jnp.dot(xn, win_ref[:, c * fc:(c + 1) * fc], preferred_element_type=F32)
        up = jnp.dot(xn, win_ref[:, D_FF + c * fc:D_FF + (c + 1) * fc], preferred_element_type=F32)
        act = (gate * jax.nn.sigmoid(gate) * up).astype(BF16)
        yc = jnp.dot(act, wout_ref[c * fc:(c + 1) * fc, :], preferred_element_type=F32)
        y = yc if y is None else y + yc
    x = x + FFN_RES_SCALE * y
    if has_final_norm:
        x = _rms(x, fg_ref[...])
    out_ref[...] = x


def _ffn(h, g, w_in, w_out, attn=None, final_g=None, name="ffn"):
    n, d = h.shape
    row = pl.BlockSpec((FFN_TILE, d), lambda i: (i, 0))
    args, specs = [h], [row]
    if attn is not None:
        o, w_o = attn
        args += [o, w_o]
        specs += [row, _resident(w_o.shape)]
    args += [g.reshape(1, d), w_in, w_out]
    specs += [_resident((1, d)), _resident(w_in.shape), _resident(w_out.shape)]
    if final_g is not None:
        args.append(final_g.reshape(1, d))
        specs.append(_resident((1, d)))
    return pl.pallas_call(
        functools.partial(_ffn_kernel, has_attn=attn is not None, has_final_norm=final_g is not None),
        out_shape=jax.ShapeDtypeStruct((n, d), F32),
        grid=(n // FFN_TILE,),
        in_specs=specs,
        out_specs=row,
        compiler_params=_params(("parallel",)),
        name=name,
    )(*args)


def _proj_kernel(h_ref, g_ref, w_ref, cos_ref, sin_ref, out_ref, *, rot_slabs, dup_heads):
    xn = _rms(h_ref[...], g_ref[...]).astype(BF16)
    cos, sin = cos_ref[...], sin_ref[...]
    lane = lax.broadcasted_iota(jnp.int32, cos.shape, 1)
    first_half = (lane % HEAD_DIM) < (HEAD_DIM // 2)
    head0 = lane < HEAD_DIM
    slabs_per_chunk = PROJ_CHUNK // LANES
    for s in range(w_ref.shape[1] // LANES):
        if s % slabs_per_chunk == 0:
            y = jnp.dot(xn, w_ref[:, s * LANES:(s + slabs_per_chunk) * LANES], preferred_element_type=F32)
        ys = y[:, (s % slabs_per_chunk) * LANES:(s % slabs_per_chunk + 1) * LANES]
        if s < rot_slabs:
            partner = jnp.where(first_half,
                                pltpu.roll(ys, LANES - HEAD_DIM // 2, 1),
                                pltpu.roll(ys, HEAD_DIM // 2, 1))
            ys = ys * cos + partner * sin
        if dup_heads:
            swapped = pltpu.roll(ys, HEAD_DIM, 1)
            out_ref[:, 2 * s * LANES:(2 * s + 1) * LANES] = jnp.where(head0, ys, swapped).astype(out_ref.dtype)
            out_ref[:, (2 * s + 1) * LANES:(2 * s + 2) * LANES] = jnp.where(head0, swapped, ys).astype(out_ref.dtype)
        else:
            out_ref[:, s * LANES:(s + 1) * LANES] = ys.astype(out_ref.dtype)


def _proj(h, g, w, cos, sin, rot_slabs, name, dup_heads=False):
    n, d = h.shape
    m = w.shape[1] * (HEADS_PER_SLAB if dup_heads else 1)
    tiles_per_seq = cos.shape[0] // TOKEN_TILE
    table = pl.BlockSpec((TOKEN_TILE, LANES), lambda i: (i % tiles_per_seq, 0))
    return pl.pallas_call(
        functools.partial(_proj_kernel, rot_slabs=rot_slabs, dup_heads=dup_heads),
        out_shape=jax.ShapeDtypeStruct((n, m), BF16),
        grid=(n // TOKEN_TILE,),
        in_specs=[pl.BlockSpec((TOKEN_TILE, d), lambda i: (i, 0)), _resident((1, d)), _resident(w.shape),
                  table, table],
        out_specs=pl.BlockSpec((TOKEN_TILE, m), lambda i: (i, 0)),
        compiler_params=_params(("parallel",)),
        name=name,
    )(h, g.reshape(1, d), w, cos, sin)


def _rotary_tables(seq):
    half = HEAD_DIM // 2
    inv_freq = ROPE_THETA ** (-jnp.arange(half, dtype=F32) / half)
    ang = jnp.arange(seq, dtype=F32)[:, None] * inv_freq[None, :]
    cos, sin = jnp.cos(ang), jnp.sin(ang)
    cos_head = jnp.concatenate([cos, cos], axis=-1)
    sin_head = jnp.concatenate([-sin, sin], axis=-1)
    return jnp.tile(cos_head, (1, HEADS_PER_SLAB)), jnp.tile(sin_head, (1, HEADS_PER_SLAB))


def _sb_kernel(*refs, n_cast):
    q_ref, k_ref, v_ref, tri_ref = refs[:4]
    cast_in = refs[4:4 + n_cast]
    o_ref = refs[4 + n_cast]
    cast_out = refs[5 + n_cast:5 + 2 * n_cast]
    mask_ref, lb_ref, split_ref, acc_ref, carry_ref, gqs_ref, gacc_ref, gcarry_ref = refs[5 + 2 * n_cast:]
    for w_ref, w_out_ref in zip(cast_in, cast_out):
        w_out_ref[...] = w_ref[...].astype(w_out_ref.dtype)

    t, band, ck, u_tiles = SB_TILE, SB_BAND, SB_CHUNK, SB_INTERLEAVE
    nck = band // ck
    seq = q_ref.shape[0]
    q_tiles = seq // t
    head_tiles = band // t
    band_tiles = q_tiles - head_tiles
    sizes = [u_tiles] * (band_tiles // u_tiles) + ([band_tiles % u_tiles] if band_tiles % u_tiles else [])
    lane = lax.broadcasted_iota(jnp.int32, (t, LANES), 1)
    head0 = lane < HEAD_DIM
    qrow = lax.broadcasted_iota(jnp.int32, (2 * t, ck), 0) & (t - 1)
    col = lax.broadcasted_iota(jnp.int32, (2 * t, ck), 1)
    mask_ref[...] = jnp.where(col < qrow + (band - t - (nck - 1) * ck), 1.0, 0.0)

    def aligned(offset):
        return offset if isinstance(offset, int) else pl.multiple_of(offset, t)

    def stacked_queries(i):
        q = q_ref[pl.ds(aligned(i * t), t), :] * ATTN_SCALE
        zero = jnp.zeros_like(q)
        return jnp.concatenate([jnp.where(head0, q, zero), jnp.where(head0, zero, q)], axis=0)

    def key_rows(start):
        return pl.ds(aligned(start), band)

    def logits(qs, start):
        z = lax.dot_general(qs, k_ref[key_rows(start), :], (((1,), (1,)), ((), ())),
                            preferred_element_type=F32)
        sp = jnp.log(1.0 + jnp.exp2(jnp.abs(z) * -LOG2E))
        log_beta = jnp.minimum(z, 0.0) - sp
        return log_beta, log_beta - z

    def split_chunk(l):
        hi = l.astype(BF16)
        lo = (l - hi.astype(F32)).astype(BF16)
        return jnp.concatenate([hi, lo], axis=1)

    def suffix_sums(split):
        return jnp.dot(split, tri_ref[...], preferred_element_type=F32)

    def chunk(x, c):
        return x[:, c * ck:(c + 1) * ck]

    def write_output(i, acc):
        o_ref[pl.ds(aligned(i * t), t), :] = jnp.where(head0, acc[:t], acc[t:]).astype(o_ref.dtype)

    def band_start(i):
        return (i - (head_tiles - 1)) * t

    def logits_stage(parity, slot, i):
        log_beta, log_1m = logits(stacked_queries(i), band_start(i))
        lb_ref[parity, slot] = log_beta
        for c in range(nck):
            l = chunk(log_1m, c)
            if c == nck - 1:
                l = l * mask_ref[...]
            split_ref[parity, slot, c * 2 * t:(c + 1) * 2 * t, :] = split_chunk(l)

    def weights_stage(parity, slot, i, sums):
        log_beta = lb_ref[parity, slot]
        later = None
        ws = [None] * nck
        for c in reversed(range(nck)):
            sums_c = sums[c * 2 * t:(c + 1) * 2 * t]
            x = chunk(log_beta, c) + chunk(sums_c, 0)
            if later is not None:
                x = x + later
            w = jnp.exp(x)
            if c == nck - 1:
                w = w * mask_ref[...]
            ws[c] = w.astype(BF16)
            later = chunk(sums_c, 1) if later is None else later + chunk(sums_c, 1)
        acc = jnp.dot(jnp.concatenate(ws, axis=1), v_ref[key_rows(band_start(i)), :],
                      preferred_element_type=F32)
        acc_ref[parity, slot] = acc
        carry_ref[parity, slot] = later
        write_output(i, acc)
        return later

    def generic_sweep(i, start, limit):
        log_beta, log_1m = logits(gqs_ref[...], start)
        bound = jnp.minimum(i * t + qrow, limit) - start
        valids = [col + c * ck < bound for c in range(nck)]
        sums = suffix_sums(jnp.concatenate(
            [split_chunk(jnp.where(valids[c], chunk(log_1m, c), 0.0)) for c in range(nck)], axis=0))
        later = gcarry_ref[...]
        ws = [None] * nck
        for c in reversed(range(nck)):
            sums_c = sums[c * 2 * t:(c + 1) * 2 * t]
            x = chunk(log_beta, c) + chunk(sums_c, 0) + later
            ws[c] = jnp.where(valids[c], jnp.exp(x), 0.0).astype(BF16)
            later = later + chunk(sums_c, 1)
        gacc_ref[...] += jnp.dot(jnp.concatenate(ws, axis=1), v_ref[key_rows(start), :],
                                 preferred_element_type=F32)
        gcarry_ref[...] = later
        return jnp.max(later)

    def head_tile(i, c):
        gqs_ref[...] = stacked_queries(i)
        gacc_ref[...] = jnp.zeros_like(gacc_ref)
        gcarry_ref[...] = jnp.zeros_like(gcarry_ref)
        generic_sweep(i, 0, seq)
        write_output(i, gacc_ref[...])
        return c

    def continue_tile(parity, slot, i):
        gqs_ref[...] = stacked_queries(i)
        gacc_ref[...] = acc_ref[parity, slot]
        gcarry_ref[...] = carry_ref[parity, slot]

        def cond(state):
            limit, top = state
            return jnp.logical_and(limit > 0, top > F32_EXP_UNDERFLOW)

        def body(state):
            limit, _ = state
            start = jnp.maximum(limit - band, 0)
            return start, generic_sweep(i, start, limit)

        lax.while_loop(cond, body, (band_start(i), jnp.max(gcarry_ref[...])))
        write_output(i, gacc_ref[...])

    def first_tile(g):
        return head_tiles + (sum(sizes[:g]) if isinstance(g, int) else g * u_tiles)

    def stage(parity, group, size, next_group, next_size):
        sums = [suffix_sums(split_ref[parity, slot]) for slot in range(size)]
        if next_group is not None:
            for slot in range(next_size):
                logits_stage(1 - parity, slot, first_tile(next_group) + slot)
        top = None
        for slot in range(size):
            carry = weights_stage(parity, slot, first_tile(group) + slot, sums[slot])
            top = carry if top is None else jnp.maximum(top, carry)

        @pl.when(jnp.max(top) > F32_EXP_UNDERFLOW)
        def _():
            def one(slot, c):
                continue_tile(parity, slot, first_tile(group) + slot)
                return c
            lax.fori_loop(0, size, one, 0)

    lax.fori_loop(0, head_tiles, head_tile, 0)
    for slot in range(sizes[0]):
        logits_stage(0, slot, first_tile(0) + slot)

    def stage_pair(m, c):
        stage(0, 2 * m, u_tiles, 2 * m + 1, u_tiles)
        stage(1, 2 * m + 1, u_tiles, 2 * m + 2, u_tiles)
        return c

    full_groups = band_tiles // u_tiles
    pairs = (full_groups - 1) // 2
    lax.fori_loop(0, pairs, stage_pair, 0)
    for g in range(2 * pairs, len(sizes)):
        has_next = g + 1 < len(sizes)
        stage(g % 2, g, sizes[g], g + 1 if has_next else None, sizes[g + 1] if has_next else 0)


def _suffix_matrix(n):
    row = lax.broadcasted_iota(jnp.int32, (n, n), 0)
    col = lax.broadcasted_iota(jnp.int32, (n, n), 1)
    half = jnp.concatenate([(row > col).astype(BF16), jnp.ones((n, n), BF16)], axis=1)
    return jnp.concatenate([half, half], axis=0)


def _sb_attention(qkv, batch, seq, weights_to_cast):
    n = qkv.shape[0]
    slabs = SB_HEADS // HEADS_PER_SLAB
    steps = batch * slabs
    tri = _suffix_matrix(SB_CHUNK)
    rows, nck = 2 * SB_TILE, SB_BAND // SB_CHUNK
    assert seq % SB_TILE == 0 and seq >= SB_BAND and SB_CHUNK == LANES
    cast_in_specs, cast_out_specs, cast_shapes = [], [], []
    for w, layer in weights_to_cast:
        w_rows, w_cols = w.shape[-2:]
        block = w_rows // steps
        assert w_rows % steps == 0 and block % 16 == 0
        if w.ndim == 2:
            cast_in_specs.append(pl.BlockSpec((block, w_cols), lambda b, s: (b * slabs + s, 0)))
        else:
            cast_in_specs.append(pl.BlockSpec((None, block, w_cols), lambda b, s, layer=layer: (layer, b * slabs + s, 0)))
        cast_out_specs.append(pl.BlockSpec((block, w_cols), lambda b, s: (b * slabs + s, 0)))
        cast_shapes.append(jax.ShapeDtypeStruct((w_rows, w_cols), BF16))
    o, *cast = pl.pallas_call(
        functools.partial(_sb_kernel, n_cast=len(weights_to_cast)),
        out_shape=[jax.ShapeDtypeStruct((n, SB_HEADS * HEAD_DIM), BF16)] + cast_shapes,
        grid=(batch, slabs),
        in_specs=[
            pl.BlockSpec((seq, LANES), lambda b, s: (b, s)),
            pl.BlockSpec((seq, LANES), lambda b, s: (b, slabs + s)),
            pl.BlockSpec((seq, LANES), lambda b, s: (b, 2 * slabs + s)),
            _resident(tri.shape),
        ] + cast_in_specs,
        out_specs=[pl.BlockSpec((seq, LANES), lambda b, s: (b, s))] + cast_out_specs,
        scratch_shapes=[
            pltpu.VMEM((rows, SB_CHUNK), F32),
            pltpu.VMEM((2, SB_INTERLEAVE, rows, SB_BAND), F32),
            pltpu.VMEM((2, SB_INTERLEAVE, nck * rows, 2 * SB_CHUNK), BF16),
            pltpu.VMEM((2, SB_INTERLEAVE, rows, LANES), F32),
            pltpu.VMEM((2, SB_INTERLEAVE, rows, LANES), F32),
            pltpu.VMEM((rows, LANES), BF16),
            pltpu.VMEM((rows, LANES), F32),
            pltpu.VMEM((rows, LANES), F32),
        ],
        compiler_params=_params(("parallel", "parallel")),
        name="sb_attention",
    )(qkv, qkv, qkv, tri, *[w for w, _ in weights_to_cast])
    return o, cast


def _swa_kernel(sink_ref, bias_ref, q_ref, kvp_ref, kvc_ref, o_ref):
    w = WINDOW
    bias = jnp.concatenate([bias_ref[0]] * SWA_GROUP, axis=0)
    lane = lax.broadcasted_iota(jnp.int32, (w, LANES), 1)
    head0 = lane < HEAD_DIM
    zero = jnp.zeros((w, LANES), BF16)
    ones = jnp.ones((2 * w, LANES), BF16)
    for h in range(SWA_KV_HEADS):
        k_cols = slice(h * LANES, (h + 1) * LANES)
        v_cols = slice((SWA_KV_HEADS + h) * LANES, (SWA_KV_HEADS + h + 1) * LANES)
        kk = jnp.concatenate([kvp_ref[:, k_cols], kvc_ref[:, k_cols]], axis=0)
        vv = jnp.concatenate([kvp_ref[:, v_cols], kvc_ref[:, v_cols]], axis=0)
        vv_ones = jnp.concatenate([vv, ones], axis=1)
        stacked, sinks = [], []
        for g in range(SWA_GROUP):
            head = h * SWA_GROUP + g
            slab = head // HEADS_PER_SLAB
            q = q_ref[:, slab * LANES:(slab + 1) * LANES] * ATTN_SCALE
            stacked.append(jnp.where(head0, q, zero) if head % HEADS_PER_SLAB == 0 else jnp.where(head0, zero, q))
            sinks.append(jnp.full((w, LANES), sink_ref[head], F32))
        qs = jnp.concatenate(stacked, axis=0)
        sink = jnp.concatenate(sinks, axis=0)
        s = lax.dot_general(qs, kk, (((1,), (1,)), ((), ())), preferred_element_type=F32) + bias
        m = jnp.maximum(jnp.broadcast_to(jnp.max(s, axis=1, keepdims=True), sink.shape), sink)
        e = jnp.exp(s - jnp.concatenate([m, m], axis=1)).astype(BF16)
        ov = jnp.dot(e, vv_ones, preferred_element_type=F32)
        denom = ov[:, LANES:] + jnp.exp(sink - m)
        o = ov[:, :LANES] * (1.0 / denom)
        for pair in range(SWA_GROUP // HEADS_PER_SLAB):
            slab = (h * SWA_GROUP) // HEADS_PER_SLAB + pair
            lo = o[(2 * pair) * w:(2 * pair + 1) * w]
            hi = o[(2 * pair + 1) * w:(2 * pair + 2) * w]
            o_ref[:, slab * LANES:(slab + 1) * LANES] = jnp.where(head0, lo, hi).astype(o_ref.dtype)


def _swa_attention(q, kv, sinks, batch, seq):
    n, dq = q.shape
    nb = seq // WINDOW
    dkv = kv.shape[1]
    qi = lax.broadcasted_iota(jnp.int32, (WINDOW, 2 * WINDOW), 0)
    ki = lax.broadcasted_iota(jnp.int32, (WINDOW, 2 * WINDOW), 1)
    diff = qi + WINDOW - ki
    in_window = (diff >= 0) & (diff < WINDOW)
    bias = jnp.where(jnp.stack([in_window & (ki >= WINDOW), in_window]), 0.0, -jnp.inf).astype(F32)
    return pl.pallas_call(
        _swa_kernel,
        out_shape=jax.ShapeDtypeStruct((n, dq), BF16),
        grid=(batch, nb),
        in_specs=[
            pl.BlockSpec(memory_space=pltpu.SMEM),
            pl.BlockSpec((1, WINDOW, 2 * WINDOW), lambda b, i: (jnp.minimum(i, 1), 0, 0)),
            pl.BlockSpec((WINDOW, dq), lambda b, i: (b * nb + i, 0)),
            pl.BlockSpec((WINDOW, dkv), lambda b, i: (b * nb + jnp.maximum(i - 1, 0), 0)),
            pl.BlockSpec((WINDOW, dkv), lambda b, i: (b * nb + i, 0)),
        ],
        out_specs=pl.BlockSpec((WINDOW, dq), lambda b, i: (b * nb + i, 0)),
        compiler_params=_params(("parallel", "arbitrary")),
        name="swa_attention",
    )(sinks, bias, q, kv, kv)


def kernel(x, ffn1_norm, ffn1_w_in, ffn1_w_out, mix_norm, ffn2_norm, ffn2_w_in, ffn2_w_out,
           sb_w_qkv, sb_w_o, kv_norm, kv_w, swa_w_q, swa_sinks, swa_w_o, final_norm):
    batch, seq, d = x.shape
    n = batch * seq
    cos, sin = _rotary_tables(seq)
    bf = _to_bf16
    h = x.reshape(n, d)

    h = _ffn(h, ffn1_norm[0], bf(ffn1_w_in, 0), bf(ffn1_w_out, 0), name="ffn1_l0")
    qkv = _proj(h, mix_norm[0], bf(sb_w_qkv, 0), cos, sin, rot_slabs=0, name="proj_qkv")
    later_weights = [(ffn2_w_in, 0), (ffn2_w_out, 0), (sb_w_o, 0), (kv_w, None), (ffn1_w_in, 1), (ffn1_w_out, 1),
                     (swa_w_q, 0), (ffn2_w_in, 1), (ffn2_w_out, 1), (swa_w_o, 0)]
    o, (w2_in0, w2_out0, w_o0, w_kv, w1_in1, w1_out1, w_q1, w2_in1, w2_out1, w_o1) = _sb_attention(
        qkv, batch, seq, later_weights)
    h = _ffn(h, ffn2_norm[0], w2_in0, w2_out0, attn=(o, w_o0), name="ffn2_l0")
    k_slabs = SWA_KV_HEADS * HEAD_DIM // LANES
    kv = _proj(h, kv_norm, w_kv, cos, sin, rot_slabs=k_slabs, name="proj_kv", dup_heads=True)

    h = _ffn(h, ffn1_norm[1], w1_in1, w1_out1, name="ffn1_l1")
    q = _proj(h, mix_norm[1], w_q1, cos, sin, rot_slabs=SWA_Q_HEADS * HEAD_DIM // LANES, name="proj_q")
    o = _swa_attention(q, kv, swa_sinks[0], batch, seq)
    h = _ffn(h, ffn2_norm[1], w2_in1, w2_out1, attn=(o, w_o1), final_g=final_norm, name="ffn2_l1")
    return h.reshape(batch, seq, d)
```

```python
import functools

import jax
import jax.numpy as jnp
from jax import lax
from jax.experimental import pallas as pl
from jax.experimental.pallas import tpu as pltpu

F32 = jnp.float32
BF16 = jnp.bfloat16

D_MODEL = 1024
HEAD_DIM = 64
SB_HEADS = 16
SWA_Q_HEADS = 16
SWA_KV_HEADS = 4
SWA_GROUP = SWA_Q_HEADS // SWA_KV_HEADS
WINDOW = 128
D_FF = 2816
ROPE_THETA = 10000.0
RMS_EPS = 1e-6
FFN_RES_SCALE = 0.5
ATTN_SCALE = HEAD_DIM ** -0.5
LOG2E = 1.4426950408889634

LANES = 128
HEADS_PER_SLAB = LANES // HEAD_DIM
VMEM_LIMIT_BYTES = 56 * 1024 * 1024

TOKEN_TILE = 1024
PROJ_CHUNK = 256
CAST_BLOCK_BYTES = 4 * 1024 * 1024
FFN_TILE = 1024
FF_CHUNKS = 11
SWA_BLOCKS_PER_STEP = 4
SB_TILE = 64
SB_BAND = 256
SB_CHUNK = 128
SB_INTERLEAVE = 8
F32_EXP_UNDERFLOW = -104.0


def _rms(x, g):
    return x * lax.rsqrt(jnp.mean(x * x, axis=-1, keepdims=True) + RMS_EPS) * g


def _params(semantics):
    return pltpu.CompilerParams(dimension_semantics=semantics, vmem_limit_bytes=VMEM_LIMIT_BYTES)


def _resident(shape):
    return pl.BlockSpec(shape, lambda *_: (0,) * len(shape), pipeline_mode=pl.Buffered(1))


def _cast_kernel(w_ref, out_ref):
    out_ref[...] = w_ref[...].astype(out_ref.dtype)


def _to_bf16(w, layer=None, name="cast"):
    rows, cols = w.shape[-2:]
    block_rows = max(8, min(rows, CAST_BLOCK_BYTES // (4 * cols) // 8 * 8))
    while rows % block_rows:
        block_rows -= 8
    if layer is None:
        in_spec = pl.BlockSpec((block_rows, cols), lambda i: (i, 0))
    else:
        in_spec = pl.BlockSpec((None, block_rows, cols), lambda i: (layer, i, 0))
    return pl.pallas_call(
        _cast_kernel,
        out_shape=jax.ShapeDtypeStruct((rows, cols), BF16),
        grid=(rows // block_rows,),
        in_specs=[in_spec],
        out_specs=pl.BlockSpec((block_rows, cols), lambda i: (i, 0)),
        compiler_params=_params(("parallel",)),
        name=name,
    )(w)


def _ffn_kernel(*refs, has_attn, has_final_norm):
    refs = list(refs)
    h_ref = refs.pop(0)
    if has_attn:
        o_ref, wo_ref = refs.pop(0), refs.pop(0)
    g_ref, win_ref, wout_ref = refs.pop(0), refs.pop(0), refs.pop(0)
    if has_final_norm:
        fg_ref = refs.pop(0)
    (out_ref,) = refs

    x = h_ref[...]
    if has_attn:
        x = x + jnp.dot(o_ref[...], wo_ref[...], preferred_element_type=F32)
    xn = _rms(x, g_ref[...]).astype(BF16)
    fc = D_FF // FF_CHUNKS
    y = None
    for c in range(FF_CHUNKS):
        gate = jnp.dot(xn, win_ref[:, c * fc:(c + 1) * fc], preferred_element_type=F32)
        up = jnp.dot(xn, win_ref[:, D_FF + c * fc:D_FF + (c + 1) * fc], preferred_element_type=F32)
        act = (gate * jax.nn.sigmoid(gate) * up).astype(BF16)
        yc = jnp.dot(act, wout_ref[c * fc:(c + 1) * fc, :], preferred_element_type=F32)
        y = yc if y is None else y + yc
    x = x + FFN_RES_SCALE * y
    if has_final_norm:
        x = _rms(x, fg_ref[...])
    out_ref[...] = x


def _ffn(h, g, w_in, w_out, attn=None, final_g=None, name="ffn"):
    n, d = h.shape
    row = pl.BlockSpec((FFN_TILE, d), lambda i: (i, 0))
    args, specs = [h], [row]
    if attn is not None:
        o, w_o = attn
        args += [o, w_o]
        specs += [row, _resident(w_o.shape)]
    args += [g.reshape(1, d), w_in, w_out]
    specs += [_resident((1, d)), _resident(w_in.shape), _resident(w_out.shape)]
    if final_g is not None:
        args.append(final_g.reshape(1, d))
        specs.append(_resident((1, d)))
    return pl.pallas_call(
        functools.partial(_ffn_kernel, has_attn=attn is not None, has_final_norm=final_g is not None),
        out_shape=jax.ShapeDtypeStruct((n, d), F32),
        grid=(n // FFN_TILE,),
        in_specs=specs,
        out_specs=row,
        compiler_params=_params(("parallel",)),
        name=name,
    )(*args)


def _proj_kernel(h_ref, g_ref, w_ref, cos_ref, sin_ref, out_ref, *, rot_slabs, dup_heads):
    xn = _rms(h_ref[...], g_ref[...]).astype(BF16)
    cos, sin = cos_ref[...], sin_ref[...]
    lane = lax.broadcasted_iota(jnp.int32, cos.shape, 1)
    first_half = (lane % HEAD_DIM) < (HEAD_DIM // 2)
    head0 = lane < HEAD_DIM
    slabs_per_chunk = PROJ_CHUNK // LANES
    for s in range(w_ref.shape[1] // LANES):
        if s % slabs_per_chunk == 0:
            y = jnp.dot(xn, w_ref[:, s * LANES:(s + slabs_per_chunk) * LANES], preferred_element_type=F32)
        ys = y[:, (s % slabs_per_chunk) * LANES:(s % slabs_per_chunk + 1) * LANES]
        if s < rot_slabs:
            partner = jnp.where(first_half,
                                pltpu.roll(ys, LANES - HEAD_DIM // 2, 1),
                                pltpu.roll(ys, HEAD_DIM // 2, 1))
            ys = ys * cos + partner * sin
        if dup_heads:
            swapped = pltpu.roll(ys, HEAD_DIM, 1)
            out_ref[:, 2 * s * LANES:(2 * s + 1) * LANES] = jnp.where(head0, ys, swapped).astype(out_ref.dtype)
            out_ref[:, (2 * s + 1) * LANES:(2 * s + 2) * LANES] = jnp.where(head0, swapped, ys).astype(out_ref.dtype)
        else:
            out_ref[:, s * LANES:(s + 1) * LANES] = ys.astype(out_ref.dtype)


def _proj(h, g, w, cos, sin, rot_slabs, name, dup_heads=False):
    n, d = h.shape
    m = w.shape[1] * (HEADS_PER_SLAB if dup_heads else 1)
    tiles_per_seq = cos.shape[0] // TOKEN_TILE
    table = pl.BlockSpec((TOKEN_TILE, LANES), lambda i: (i % tiles_per_seq, 0))
    return pl.pallas_call(
        functools.partial(_proj_kernel, rot_slabs=rot_slabs, dup_heads=dup_heads),
        out_shape=jax.ShapeDtypeStruct((n, m), BF16),
        grid=(n // TOKEN_TILE,),
        in_specs=[pl.BlockSpec((TOKEN_TILE, d), lambda i: (i, 0)), _resident((1, d)), _resident(w.shape),
                  table, table],
        out_specs=pl.BlockSpec((TOKEN_TILE, m), lambda i: (i, 0)),
        compiler_params=_params(("parallel",)),
        name=name,
    )(h, g.reshape(1, d), w, cos, sin)


def _rotary_tables(seq):
    half = HEAD_DIM // 2
    inv_freq = ROPE_THETA ** (-jnp.arange(half, dtype=F32) / half)
    ang = jnp.arange(seq, dtype=F32)[:, None] * inv_freq[None, :]
    cos, sin = jnp.cos(ang), jnp.sin(ang)
    cos_head = jnp.concatenate([cos, cos], axis=-1)
    sin_head = jnp.concatenate([-sin, sin], axis=-1)
    return jnp.tile(cos_head, (1, HEADS_PER_SLAB)), jnp.tile(sin_head, (1, HEADS_PER_SLAB))


def _sb_kernel(*refs, n_cast):
    q_ref, k_ref, v_ref, tri_ref = refs[:4]
    cast_in = refs[4:4 + n_cast]
    o_ref = refs[4 + n_cast]
    cast_out = refs[5 + n_cast:5 + 2 * n_cast]
    mask_ref, lb_ref, split_ref, acc_ref, carry_ref, gqs_ref, gacc_ref, gcarry_ref = refs[5 + 2 * n_cast:]
    for w_ref, w_out_ref in zip(cast_in, cast_out):
        w_out_ref[...] = w_ref[...].astype(w_out_ref.dtype)

    t, band, ck, u_tiles = SB_TILE, SB_BAND, SB_CHUNK, SB_INTERLEAVE
    nck = band // ck
    seq = q_ref.shape[0]
    q_tiles = seq // t
    head_tiles = band // t
    band_tiles = q_tiles - head_tiles
    sizes = [u_tiles] * (band_tiles // u_tiles) + ([band_tiles % u_tiles] if band_tiles % u_tiles else [])
    lane = lax.broadcasted_iota(jnp.int32, (t, LANES), 1)
    head0 = lane < HEAD_DIM
    qrow = lax.broadcasted_iota(jnp.int32, (2 * t, ck), 0) & (t - 1)
    col = lax.broadcasted_iota(jnp.int32, (2 * t, ck), 1)
    mask_ref[...] = jnp.where(col < qrow + (band - t - (nck - 1) * ck), 1.0, 0.0)

    def aligned(offset):
        return offset if isinstance(offset, int) else pl.multiple_of(offset, t)

    def stacked_queries(i):
        q = q_ref[pl.ds(aligned(i * t), t), :] * ATTN_SCALE
        zero = jnp.zeros_like(q)
        return jnp.concatenate([jnp.where(head0, q, zero), jnp.where(head0, zero, q)], axis=0)

    def key_rows(start):
        return pl.ds(aligned(start), band)

    def logits(qs, start):
        z = lax.dot_general(qs, k_ref[key_rows(start), :], (((1,), (1,)), ((), ())),
                            preferred_element_type=F32)
        sp = jnp.log(1.0 + jnp.exp2(jnp.abs(z) * -LOG2E))
        log_beta = jnp.minimum(z, 0.0) - sp
        return log_beta, log_beta - z

    def split_chunk(l):
        hi = l.astype(BF16)
        lo = (l - hi.astype(F32)).astype(BF16)
        return jnp.concatenate([hi, lo], axis=1)

    def suffix_sums(split):
        return jnp.dot(split, tri_ref[...], preferred_element_type=F32)

    def chunk(x, c):
        return x[:, c * ck:(c + 1) * ck]

    def write_output(i, acc):
        o_ref[pl.ds(aligned(i * t), t), :] = jnp.where(head0, acc[:t], acc[t:]).astype(o_ref.dtype)

    def band_start(i):
        return (i - (head_tiles - 1)) * t

    def logits_stage(parity, slot, i):
        log_beta, log_1m = logits(stacked_queries(i), band_start(i))
        lb_ref[parity, slot] = log_beta
        for c in range(nck):
            l = chunk(log_1m, c)
            if c == nck - 1:
                l = l * mask_ref[...]
            split_ref[parity, slot, c * 2 * t:(c + 1) * 2 * t, :] = split_chunk(l)

    def weights_stage(parity, slot, i, sums):
        log_beta = lb_ref[parity, slot]
        later = None
        ws = [None] * nck
        for c in reversed(range(nck)):
            sums_c = sums[c * 2 * t:(c + 1) * 2 * t]
            x = chunk(log_beta, c) + chunk(sums_c, 0)
            if later is not None:
                x = x + later
            w = jnp.exp(x)
            if c == nck - 1:
                w = w * mask_ref[...]
            ws[c] = w.astype(BF16)
            later = chunk(sums_c, 1) if later is None else later + chunk(sums_c, 1)
        acc = jnp.dot(jnp.concatenate(ws, axis=1), v_ref[key_rows(band_start(i)), :],
                      preferred_element_type=F32)
        acc_ref[parity, slot] = acc
        carry_ref[parity, slot] = later
        write_output(i, acc)
        return later

    def generic_sweep(i, start, limit):
        log_beta, log_1m = logits(gqs_ref[...], start)
        bound = jnp.minimum(i * t + qrow, limit) - start
        valids = [col + c * ck < bound for c in range(nck)]
        sums = suffix_sums(jnp.concatenate(
            [split_chunk(jnp.where(valids[c], chunk(log_1m, c), 0.0)) for c in range(nck)], axis=0))
        later = gcarry_ref[...]
        ws = [None] * nck
        for c in reversed(range(nck)):
            sums_c = sums[c * 2 * t:(c + 1) * 2 * t]
            x = chunk(log_beta, c) + chunk(sums_c, 0) + later
            ws[c] = jnp.where(valids[c], jnp.exp(x), 0.0).astype(BF16)
            later = later + chunk(sums_c, 1)
        gacc_ref[...] += jnp.dot(jnp.concatenate(ws, axis=1), v_ref[key_rows(start), :],
                                 preferred_element_type=F32)
        gcarry_ref[...] = later
        return jnp.max(later)

    def head_tile(i, c):
        gqs_ref[...] = stacked_queries(i)
        gacc_ref[...] = jnp.zeros_like(gacc_ref)
        gcarry_ref[...] = jnp.zeros_like(gcarry_ref)
        generic_sweep(i, 0, seq)
        write_output(i, gacc_ref[...])
        return c

    def continue_tile(parity, slot, i):
        gqs_ref[...] = stacked_queries(i)
        gacc_ref[...] = acc_ref[parity, slot]
        gcarry_ref[...] = carry_ref[parity, slot]

        def cond(state):
            limit, top = state
            return jnp.logical_and(limit > 0, top > F32_EXP_UNDERFLOW)

        def body(state):
            limit, _ = state
            start = jnp.maximum(limit - band, 0)
            return start, generic_sweep(i, start, limit)

        lax.while_loop(cond, body, (band_start(i), jnp.max(gcarry_ref[...])))
        write_output(i, gacc_ref[...])

    def first_tile(g):
        return head_tiles + (sum(sizes[:g]) if isinstance(g, int) else g * u_tiles)

    def stage(parity, group, size, next_group, next_size):
        sums = [suffix_sums(split_ref[parity, slot]) for slot in range(size)]
        if next_group is not None:
            for slot in range(next_size):
                logits_stage(1 - parity, slot, first_tile(next_group) + slot)
        top = None
        for slot in range(size):
            carry = weights_stage(parity, slot, first_tile(group) + slot, sums[slot])
            top = carry if top is None else jnp.maximum(top, carry)

        @pl.when(jnp.max(top) > F32_EXP_UNDERFLOW)
        def _():
            def one(slot, c):
                continue_tile(parity, slot, first_tile(group) + slot)
                return c
            lax.fori_loop(0, size, one, 0)

    lax.fori_loop(0, head_tiles, head_tile, 0)
    for slot in range(sizes[0]):
        logits_stage(0, slot, first_tile(0) + slot)

    def stage_pair(m, c):
        stage(0, 2 * m, u_tiles, 2 * m + 1, u_tiles)
        stage(1, 2 * m + 1, u_tiles, 2 * m + 2, u_tiles)
        return c

    full_groups = band_tiles // u_tiles
    pairs = (full_groups - 1) // 2
    lax.fori_loop(0, pairs, stage_pair, 0)
    for g in range(2 * pairs, len(sizes)):
        has_next = g + 1 < len(sizes)
        stage(g % 2, g, sizes[g], g + 1 if has_next else None, sizes[g + 1] if has_next else 0)


def _suffix_matrix(n):
    row = lax.broadcasted_iota(jnp.int32, (n, n), 0)
    col = lax.broadcasted_iota(jnp.int32, (n, n), 1)
    half = jnp.concatenate([(row > col).astype(BF16), jnp.ones((n, n), BF16)], axis=1)
    return jnp.concatenate([half, half], axis=0)


def _sb_attention(qkv, batch, seq, weights_to_cast):
    n = qkv.shape[0]
    slabs = SB_HEADS // HEADS_PER_SLAB
    steps = batch * slabs
    tri = _suffix_matrix(SB_CHUNK)
    rows, nck = 2 * SB_TILE, SB_BAND // SB_CHUNK
    assert seq % SB_TILE == 0 and seq >= SB_BAND and SB_CHUNK == LANES
    cast_in_specs, cast_out_specs, cast_shapes = [], [], []
    for w, layer in weights_to_cast:
        w_rows, w_cols = w.shape[-2:]
        block = w_rows // steps
        assert w_rows % steps == 0 and block % 16 == 0
        if w.ndim == 2:
            cast_in_specs.append(pl.BlockSpec((block, w_cols), lambda b, s: (b * slabs + s, 0)))
        else:
            cast_in_specs.append(pl.BlockSpec((None, block, w_cols), lambda b, s, layer=layer: (layer, b * slabs + s, 0)))
        cast_out_specs.append(pl.BlockSpec((block, w_cols), lambda b, s: (b * slabs + s, 0)))
        cast_shapes.append(jax.ShapeDtypeStruct((w_rows, w_cols), BF16))
    o, *cast = pl.pallas_call(
        functools.partial(_sb_kernel, n_cast=len(weights_to_cast)),
        out_shape=[jax.ShapeDtypeStruct((n, SB_HEADS * HEAD_DIM), BF16)] + cast_shapes,
        grid=(batch, slabs),
        in_specs=[
            pl.BlockSpec((seq, LANES), lambda b, s: (b, s)),
            pl.BlockSpec((seq, LANES), lambda b, s: (b, slabs + s)),
            pl.BlockSpec((seq, LANES), lambda b, s: (b, 2 * slabs + s)),
            _resident(tri.shape),
        ] + cast_in_specs,
        out_specs=[pl.BlockSpec((seq, LANES), lambda b, s: (b, s))] + cast_out_specs,
        scratch_shapes=[
            pltpu.VMEM((rows, SB_CHUNK), F32),
            pltpu.VMEM((2, SB_INTERLEAVE, rows, SB_BAND), F32),
            pltpu.VMEM((2, SB_INTERLEAVE, nck * rows, 2 * SB_CHUNK), BF16),
            pltpu.VMEM((2, SB_INTERLEAVE, rows, LANES), F32),
            pltpu.VMEM((2, SB_INTERLEAVE, rows, LANES), F32),
            pltpu.VMEM((rows, LANES), BF16),
            pltpu.VMEM((rows, LANES), F32),
            pltpu.VMEM((rows, LANES), F32),
        ],
        compiler_params=_params(("parallel", "parallel")),
        name="sb_attention",
    )(qkv, qkv, qkv, tri, *[w for w, _ in weights_to_cast])
    return o, cast


def _swa_kernel(sink_ref, bias_first_ref, bias_ref, q_ref, kvp_ref, kvc_ref, o_ref):
    w = WINDOW
    lane = lax.broadcasted_iota(jnp.int32, (w, LANES), 1)
    head0 = lane < HEAD_DIM
    zero = jnp.zeros((w, LANES), BF16)
    ones = jnp.ones((2 * w, LANES), BF16)
    for blk in range(SWA_BLOCKS_PER_STEP):
        rows = slice(blk * w, (blk + 1) * w)
        bias = jnp.concatenate([(bias_first_ref if blk == 0 else bias_ref)[0]] * SWA_GROUP, axis=0)
        for h in range(SWA_KV_HEADS):
            k_cols = slice(h * LANES, (h + 1) * LANES)
            v_cols = slice((SWA_KV_HEADS + h) * LANES, (SWA_KV_HEADS + h + 1) * LANES)
            if blk == 0:
                prev_k, prev_v = kvp_ref[:, k_cols], kvp_ref[:, v_cols]
            else:
                prev_k, prev_v = kvc_ref[(blk - 1) * w:blk * w, k_cols], kvc_ref[(blk - 1) * w:blk * w, v_cols]
            kk = jnp.concatenate([prev_k, kvc_ref[rows, k_cols]], axis=0)
            vv = jnp.concatenate([prev_v, kvc_ref[rows, v_cols]], axis=0)
            vv_ones = jnp.concatenate([vv, ones], axis=1)
            stacked, sinks = [], []
            for g in range(SWA_GROUP):
                head = h * SWA_GROUP + g
                slab = head // HEADS_PER_SLAB
                q = q_ref[rows, slab * LANES:(slab + 1) * LANES] * ATTN_SCALE
                stacked.append(jnp.where(head0, q, zero) if head % HEADS_PER_SLAB == 0 else jnp.where(head0, zero, q))
                sinks.append(jnp.full((w, LANES), sink_ref[head], F32))
            qs = jnp.concatenate(stacked, axis=0)
            sink = jnp.concatenate(sinks, axis=0)
            s = lax.dot_general(qs, kk, (((1,), (1,)), ((), ())), preferred_element_type=F32) + bias
            m = jnp.maximum(jnp.broadcast_to(jnp.max(s, axis=1, keepdims=True), sink.shape), sink)
            e = jnp.exp(s - jnp.concatenate([m, m], axis=1)).astype(BF16)
            ov = jnp.dot(e, vv_ones, preferred_element_type=F32)
            for pair in range(SWA_GROUP // HEADS_PER_SLAB):
                slab = (h * SWA_GROUP) // HEADS_PER_SLAB + pair
                lo, hi = slice(2 * pair * w, (2 * pair + 1) * w), slice((2 * pair + 1) * w, (2 * pair + 2) * w)
                both = lambda x: jnp.where(head0, x[lo], x[hi])
                denom = both(ov[:, LANES:]) + jnp.exp(both(sink) - both(m))
                o_ref[rows, slab * LANES:(slab + 1) * LANES] = (both(ov[:, :LANES]) * (1.0 / denom)).astype(o_ref.dtype)


def _swa_attention(q, kv, sinks, batch, seq):
    n, dq = q.shape
    step_rows = SWA_BLOCKS_PER_STEP * WINDOW
    steps = seq // step_rows
    dkv = kv.shape[1]
    qi = lax.broadcasted_iota(jnp.int32, (WINDOW, 2 * WINDOW), 0)
    ki = lax.broadcasted_iota(jnp.int32, (WINDOW, 2 * WINDOW), 1)
    diff = qi + WINDOW - ki
    in_window = (diff >= 0) & (diff < WINDOW)
    bias = jnp.where(jnp.stack([in_window & (ki >= WINDOW), in_window]), 0.0, -jnp.inf).astype(F32)
    bias_spec = lambda index: pl.BlockSpec((1, WINDOW, 2 * WINDOW), index)
    return pl.pallas_call(
        _swa_kernel,
        out_shape=jax.ShapeDtypeStruct((n, dq), BF16),
        grid=(batch, steps),
        in_specs=[
            pl.BlockSpec(memory_space=pltpu.SMEM),
            bias_spec(lambda b, i: (jnp.minimum(i, 1), 0, 0)),
            bias_spec(lambda b, i: (1, 0, 0)),
            pl.BlockSpec((step_rows, dq), lambda b, i: (b * steps + i, 0)),
            pl.BlockSpec((WINDOW, dkv),
                         lambda b, i: ((b * steps + i) * SWA_BLOCKS_PER_STEP - jnp.minimum(i, 1), 0)),
            pl.BlockSpec((step_rows, dkv), lambda b, i: (b * steps + i, 0)),
        ],
        out_specs=pl.BlockSpec((step_rows, dq), lambda b, i: (b * steps + i, 0)),
        compiler_params=_params(("parallel", "arbitrary")),
        name="swa_attention",
    )(sinks, bias, bias, q, kv, kv)


def kernel(x, ffn1_norm, ffn1_w_in, ffn1_w_out, mix_norm, ffn2_norm, ffn2_w_in, ffn2_w_out,
           sb_w_qkv, sb_w_o, kv_norm, kv_w, swa_w_q, swa_sinks, swa_w_o, final_norm):
    batch, seq, d = x.shape
    n = batch * seq
    cos, sin = _rotary_tables(seq)
    bf = _to_bf16
    h = x.reshape(n, d)

    h = _ffn(h, ffn1_norm[0], bf(ffn1_w_in, 0), bf(ffn1_w_out, 0), name="ffn1_l0")
    qkv = _proj(h, mix_norm[0], bf(sb_w_qkv, 0), cos, sin, rot_slabs=0, name="proj_qkv")
    later_weights = [(ffn2_w_in, 0), (ffn2_w_out, 0), (sb_w_o, 0), (kv_w, None), (ffn1_w_in, 1), (ffn1_w_out, 1),
                     (swa_w_q, 0), (ffn2_w_in, 1), (ffn2_w_out, 1), (swa_w_o, 0)]
    o, (w2_in0, w2_out0, w_o0, w_kv, w1_in1, w1_out1, w_q1, w2_in1, w2_out1, w_o1) = _sb_attention(
        qkv, batch, seq, later_weights)
    h = _ffn(h, ffn2_norm[0], w2_in0, w2_out0, attn=(o, w_o0), name="ffn2_l0")
    k_slabs = SWA_KV_HEADS * HEAD_DIM // LANES
    kv = _proj(h, kv_norm, w_kv, cos, sin, rot_slabs=k_slabs, name="proj_kv", dup_heads=True)

    h = _ffn(h, ffn1_norm[1], w1_in1, w1_out1, name="ffn1_l1")
    q = _proj(h, mix_norm[1], w_q1, cos, sin, rot_slabs=SWA_Q_HEADS * HEAD_DIM // LANES, name="proj_q")
    o = _swa_attention(q, kv, swa_sinks[0], batch, seq)
    h = _ffn(h, ffn2_norm[1], w2_in1, w2_out1, attn=(o, w_o1), final_g=final_norm, name="ffn2_l1")
    return h.reshape(batch, seq, d)
```

```python
import functools

import jax
import jax.numpy as jnp
from jax import lax
from jax.experimental import pallas as pl
from jax.experimental.pallas import tpu as pltpu

F32 = jnp.float32
BF16 = jnp.bfloat16

D_MODEL = 1024
HEAD_DIM = 64
SB_HEADS = 16
SWA_Q_HEADS = 16
SWA_KV_HEADS = 4
SWA_GROUP = SWA_Q_HEADS // SWA_KV_HEADS
WINDOW = 128
D_FF = 2816
ROPE_THETA = 10000.0
RMS_EPS = 1e-6
FFN_RES_SCALE = 0.5
ATTN_SCALE = HEAD_DIM ** -0.5
LOG2E = 1.4426950408889634

LANES = 128
HEADS_PER_SLAB = LANES // HEAD_DIM
VMEM_LIMIT_BYTES = 56 * 1024 * 1024

TOKEN_TILE = 1024
PROJ_CHUNK = 256
CAST_BLOCK_BYTES = 4 * 1024 * 1024
FFN_TILE = 1024
FF_CHUNKS = 11
SWA_BLOCKS_PER_STEP = 4
SB_TILE = 64
SB_BAND = 256
SB_CHUNK = 128
SB_INTERLEAVE = 16
F32_EXP_UNDERFLOW = -104.0


def _rms(x, g):
    return x * lax.rsqrt(jnp.mean(x * x, axis=-1, keepdims=True) + RMS_EPS) * g


def _params(semantics):
    return pltpu.CompilerParams(dimension_semantics=semantics, vmem_limit_bytes=VMEM_LIMIT_BYTES)


def _resident(shape):
    return pl.BlockSpec(shape, lambda *_: (0,) * len(shape), pipeline_mode=pl.Buffered(1))


def _cast_kernel(w_ref, out_ref):
    out_ref[...] = w_ref[...].astype(out_ref.dtype)


def _to_bf16(w, layer=None, name="cast"):
    rows, cols = w.shape[-2:]
    block_rows = max(8, min(rows, CAST_BLOCK_BYTES // (4 * cols) // 8 * 8))
    while rows % block_rows:
        block_rows -= 8
    if layer is None:
        in_spec = pl.BlockSpec((block_rows, cols), lambda i: (i, 0))
    else:
        in_spec = pl.BlockSpec((None, block_rows, cols), lambda i: (layer, i, 0))
    return pl.pallas_call(
        _cast_kernel,
        out_shape=jax.ShapeDtypeStruct((rows, cols), BF16),
        grid=(rows // block_rows,),
        in_specs=[in_spec],
        out_specs=pl.BlockSpec((block_rows, cols), lambda i: (i, 0)),
        compiler_params=_params(("parallel",)),
        name=name,
    )(w)


def _ffn_kernel(*refs, has_attn, has_final_norm, has_side_cast):
    refs = list(refs)
    h_ref = refs.pop(0)
    if has_attn:
        o_ref, wo_ref = refs.pop(0), refs.pop(0)
    g_ref, win_ref, wout_ref = refs.pop(0), refs.pop(0), refs.pop(0)
    if has_final_norm:
        fg_ref = refs.pop(0)
    if has_side_cast:
        side_ref, out_ref, side_out_ref = refs
        side_out_ref[...] = side_ref[...].astype(side_out_ref.dtype)
    else:
        (out_ref,) = refs

    x = h_ref[...]
    if has_attn:
        x = x + jnp.dot(o_ref[...], wo_ref[...], preferred_element_type=F32)
    xn = _rms(x, g_ref[...]).astype(BF16)
    fc = D_FF // FF_CHUNKS
    y = None
    for c in range(FF_CHUNKS):
        gate = jnp.dot(xn, win_ref[:, c * fc:(c + 1) * fc], preferred_element_type=F32)
        up = jnp.dot(xn, win_ref[:, D_FF + c * fc:D_FF + (c + 1) * fc], preferred_element_type=F32)
        act = (gate * jax.nn.sigmoid(gate) * up).astype(BF16)
        yc = jnp.dot(act, wout_ref[c * fc:(c + 1) * fc, :], preferred_element_type=F32)
        y = yc if y is None else y + yc
    x = x + FFN_RES_SCALE * y
    if has_final_norm:
        x = _rms(x, fg_ref[...])
    out_ref[...] = x


def _ffn(h, g, w_in, w_out, attn=None, final_g=None, side_cast=None, name="ffn"):
    n, d = h.shape
    steps = n // FFN_TILE
    row = pl.BlockSpec((FFN_TILE, d), lambda i: (i, 0))
    args, specs = [h], [row]
    if attn is not None:
        o, w_o = attn
        args += [o, w_o]
        specs += [row, _resident(w_o.shape)]
    args += [g.reshape(1, d), w_in, w_out]
    specs += [_resident((1, d)), _resident(w_in.shape), _resident(w_out.shape)]
    if final_g is not None:
        args.append(final_g.reshape(1, d))
        specs.append(_resident((1, d)))
    out_shape, out_specs = jax.ShapeDtypeStruct((n, d), F32), row
    if side_cast is not None:
        w, layer = side_cast
        w_rows, w_cols = w.shape[-2:]
        block = w_rows // steps
        assert w_rows % steps == 0 and block % 16 == 0
        args.append(w)
        specs.append(pl.BlockSpec((None, block, w_cols), lambda i: (layer, i, 0)))
        out_shape = [out_shape, jax.ShapeDtypeStruct((w_rows, w_cols), BF16)]
        out_specs = [row, pl.BlockSpec((block, w_cols), lambda i: (i, 0))]
    return pl.pallas_call(
        functools.partial(_ffn_kernel, has_attn=attn is not None, has_final_norm=final_g is not None,
                          has_side_cast=side_cast is not None),
        out_shape=out_shape,
        grid=(steps,),
        in_specs=specs,
        out_specs=out_specs,
        compiler_params=_params(("parallel",)),
        name=name,
    )(*args)


def _proj_kernel(h_ref, g_ref, w_ref, cos_ref, sin_ref, out_ref, *, rot_slabs, dup_heads):
    xn = _rms(h_ref[...], g_ref[...]).astype(BF16)
    cos, sin = cos_ref[...], sin_ref[...]
    lane = lax.broadcasted_iota(jnp.int32, cos.shape, 1)
    first_half = (lane % HEAD_DIM) < (HEAD_DIM // 2)
    head0 = lane < HEAD_DIM
    slabs_per_chunk = PROJ_CHUNK // LANES
    for s in range(w_ref.shape[1] // LANES):
        if s % slabs_per_chunk == 0:
            y = jnp.dot(xn, w_ref[:, s * LANES:(s + slabs_per_chunk) * LANES], preferred_element_type=F32)
        ys = y[:, (s % slabs_per_chunk) * LANES:(s % slabs_per_chunk + 1) * LANES]
        if s < rot_slabs:
            partner = jnp.where(first_half,
                                pltpu.roll(ys, LANES - HEAD_DIM // 2, 1),
                                pltpu.roll(ys, HEAD_DIM // 2, 1))
            ys = ys * cos + partner * sin
        if dup_heads:
            swapped = pltpu.roll(ys, HEAD_DIM, 1)
            out_ref[:, 2 * s * LANES:(2 * s + 1) * LANES] = jnp.where(head0, ys, swapped).astype(out_ref.dtype)
            out_ref[:, (2 * s + 1) * LANES:(2 * s + 2) * LANES] = jnp.where(head0, swapped, ys).astype(out_ref.dtype)
        else:
            out_ref[:, s * LANES:(s + 1) * LANES] = ys.astype(out_ref.dtype)


def _proj(h, g, w, cos, sin, rot_slabs, name, dup_heads=False):
    n, d = h.shape
    m = w.shape[1] * (HEADS_PER_SLAB if dup_heads else 1)
    tiles_per_seq = cos.shape[0] // TOKEN_TILE
    table = pl.BlockSpec((TOKEN_TILE, LANES), lambda i: (i % tiles_per_seq, 0))
    return pl.pallas_call(
        functools.partial(_proj_kernel, rot_slabs=rot_slabs, dup_heads=dup_heads),
        out_shape=jax.ShapeDtypeStruct((n, m), BF16),
        grid=(n // TOKEN_TILE,),
        in_specs=[pl.BlockSpec((TOKEN_TILE, d), lambda i: (i, 0)), _resident((1, d)), _resident(w.shape),
                  table, table],
        out_specs=pl.BlockSpec((TOKEN_TILE, m), lambda i: (i, 0)),
        compiler_params=_params(("parallel",)),
        name=name,
    )(h, g.reshape(1, d), w, cos, sin)


def _rotary_tables(seq):
    half = HEAD_DIM // 2
    lane = lax.broadcasted_iota(jnp.int32, (seq, LANES), 1)
    pos = lax.broadcasted_iota(jnp.int32, (seq, LANES), 0).astype(F32)
    inv_freq = ROPE_THETA ** (-(lane % half).astype(F32) / half)
    ang = pos * inv_freq
    first_half = (lane % HEAD_DIM) < half
    return jnp.cos(ang), jnp.where(first_half, -jnp.sin(ang), jnp.sin(ang))


def _sb_kernel(*refs, n_cast):
    q_ref, k_ref, v_ref, tri_ref = refs[:4]
    cast_in = refs[4:4 + n_cast]
    o_ref = refs[4 + n_cast]
    cast_out = refs[5 + n_cast:5 + 2 * n_cast]
    mask_ref, lb_ref, split_ref, acc_ref, carry_ref, gqs_ref, gacc_ref, gcarry_ref = refs[5 + 2 * n_cast:]
    for w_ref, w_out_ref in zip(cast_in, cast_out):
        w_out_ref[...] = w_ref[...].astype(w_out_ref.dtype)

    t, band, ck, u_tiles = SB_TILE, SB_BAND, SB_CHUNK, SB_INTERLEAVE
    nck = band // ck
    seq = q_ref.shape[0]
    q_tiles = seq // t
    head_tiles = band // t
    band_tiles = q_tiles - head_tiles
    sizes = [u_tiles] * (band_tiles // u_tiles) + ([band_tiles % u_tiles] if band_tiles % u_tiles else [])
    lane = lax.broadcasted_iota(jnp.int32, (t, LANES), 1)
    head0 = lane < HEAD_DIM
    qrow = lax.broadcasted_iota(jnp.int32, (2 * t, ck), 0) & (t - 1)
    col = lax.broadcasted_iota(jnp.int32, (2 * t, ck), 1)
    mask_ref[...] = jnp.where(col < qrow + (band - t - (nck - 1) * ck), 1.0, 0.0)

    def aligned(offset):
        return offset if isinstance(offset, int) else pl.multiple_of(offset, t)

    def stacked_queries(i):
        q = q_ref[pl.ds(aligned(i * t), t), :] * ATTN_SCALE
        zero = jnp.zeros_like(q)
        return jnp.concatenate([jnp.where(head0, q, zero), jnp.where(head0, zero, q)], axis=0)

    def key_rows(start):
        return pl.ds(aligned(start), band)

    def logits(qs, start):
        z = lax.dot_general(qs, k_ref[key_rows(start), :], (((1,), (1,)), ((), ())),
                            preferred_element_type=F32)
        sp = jnp.log(1.0 + jnp.exp2(jnp.abs(z) * -LOG2E))
        log_beta = jnp.minimum(z, 0.0) - sp
        return log_beta, log_beta - z

    def split_chunk(l):
        hi = l.astype(BF16)
        lo = (l - hi.astype(F32)).astype(BF16)
        return jnp.concatenate([hi, lo], axis=1)

    def suffix_sums(split):
        return jnp.dot(split, tri_ref[...], preferred_element_type=F32)

    def chunk(x, c):
        return x[:, c * ck:(c + 1) * ck]

    def write_output(i, acc):
        o_ref[pl.ds(aligned(i * t), t), :] = jnp.where(head0, acc[:t], acc[t:]).astype(o_ref.dtype)

    def band_start(i):
        return (i - (head_tiles - 1)) * t

    def logits_stage(parity, slot, i):
        log_beta, log_1m = logits(stacked_queries(i), band_start(i))
        lb_ref[parity, slot] = log_beta
        for c in range(nck):
            l = chunk(log_1m, c)
            if c == nck - 1:
                l = l * mask_ref[...]
            split_ref[parity, slot, c * 2 * t:(c + 1) * 2 * t, :] = split_chunk(l)

    def weights_stage(parity, slot, i, sums):
        log_beta = lb_ref[parity, slot]
        later = None
        ws = [None] * nck
        for c in reversed(range(nck)):
            sums_c = sums[c * 2 * t:(c + 1) * 2 * t]
            x = chunk(log_beta, c) + chunk(sums_c, 0)
            if later is not None:
                x = x + later
            w = jnp.exp(x)
            if c == nck - 1:
                w = w * mask_ref[...]
            ws[c] = w.astype(BF16)
            later = chunk(sums_c, 1) if later is None else later + chunk(sums_c, 1)
        acc = jnp.dot(jnp.concatenate(ws, axis=1), v_ref[key_rows(band_start(i)), :],
                      preferred_element_type=F32)
        acc_ref[parity, slot] = acc
        carry_ref[parity, slot] = later
        write_output(i, acc)
        return later

    def generic_sweep(i, start, limit):
        log_beta, log_1m = logits(gqs_ref[...], start)
        bound = jnp.minimum(i * t + qrow, limit) - start
        valids = [col + c * ck < bound for c in range(nck)]
        sums = suffix_sums(jnp.concatenate(
            [split_chunk(jnp.where(valids[c], chunk(log_1m, c), 0.0)) for c in range(nck)], axis=0))
        later = gcarry_ref[...]
        ws = [None] * nck
        for c in reversed(range(nck)):
            sums_c = sums[c * 2 * t:(c + 1) * 2 * t]
            x = chunk(log_beta, c) + chunk(sums_c, 0) + later
            ws[c] = jnp.where(valids[c], jnp.exp(x), 0.0).astype(BF16)
            later = later + chunk(sums_c, 1)
        gacc_ref[...] += jnp.dot(jnp.concatenate(ws, axis=1), v_ref[key_rows(start), :],
                                 preferred_element_type=F32)
        gcarry_ref[...] = later
        return jnp.max(later)

    def head_tile(i, c):
        gqs_ref[...] = stacked_queries(i)
        gacc_ref[...] = jnp.zeros_like(gacc_ref)
        gcarry_ref[...] = jnp.zeros_like(gcarry_ref)
        generic_sweep(i, 0, seq)
        write_output(i, gacc_ref[...])
        return c

    def continue_tile(parity, slot, i):
        gqs_ref[...] = stacked_queries(i)
        gacc_ref[...] = acc_ref[parity, slot]
        gcarry_ref[...] = carry_ref[parity, slot]

        def cond(state):
            limit, top = state
            return jnp.logical_and(limit > 0, top > F32_EXP_UNDERFLOW)

        def body(state):
            limit, _ = state
            start = jnp.maximum(limit - band, 0)
            return start, generic_sweep(i, start, limit)

        lax.while_loop(cond, body, (band_start(i), jnp.max(gcarry_ref[...])))
        write_output(i, gacc_ref[...])

    def first_tile(g):
        return head_tiles + (sum(sizes[:g]) if isinstance(g, int) else g * u_tiles)

    def stage(parity, group, size, next_group, next_size):
        sums = [suffix_sums(split_ref[parity, slot]) for slot in range(size)]
        if next_group is not None:
            for slot in range(next_size):
                logits_stage(1 - parity, slot, first_tile(next_group) + slot)
        top = None
        for slot in range(size):
            carry = weights_stage(parity, slot, first_tile(group) + slot, sums[slot])
            top = carry if top is None else jnp.maximum(top, carry)

        @pl.when(jnp.max(top) > F32_EXP_UNDERFLOW)
        def _():
            def one(slot, c):
                continue_tile(parity, slot, first_tile(group) + slot)
                return c
            lax.fori_loop(0, size, one, 0)

    lax.fori_loop(0, head_tiles, head_tile, 0)
    for slot in range(sizes[0]):
        logits_stage(0, slot, first_tile(0) + slot)

    def stage_pair(m, c):
        stage(0, 2 * m, u_tiles, 2 * m + 1, u_tiles)
        stage(1, 2 * m + 1, u_tiles, 2 * m + 2, u_tiles)
        return c

    full_groups = band_tiles // u_tiles
    pairs = max(full_groups - 1, 0) // 2
    lax.fori_loop(0, pairs, stage_pair, 0)
    for g in range(2 * pairs, len(sizes)):
        has_next = g + 1 < len(sizes)
        stage(g % 2, g, sizes[g], g + 1 if has_next else None, sizes[g + 1] if has_next else 0)


def _suffix_matrix(n):
    row = lax.broadcasted_iota(jnp.int32, (n, n), 0)
    col = lax.broadcasted_iota(jnp.int32, (n, n), 1)
    half = jnp.concatenate([(row > col).astype(BF16), jnp.ones((n, n), BF16)], axis=1)
    return jnp.concatenate([half, half], axis=0)


def _sb_attention(qkv, batch, seq, weights_to_cast):
    n = qkv.shape[0]
    slabs = SB_HEADS // HEADS_PER_SLAB
    steps = batch * slabs
    tri = _suffix_matrix(SB_CHUNK)
    rows, nck = 2 * SB_TILE, SB_BAND // SB_CHUNK
    assert seq % SB_TILE == 0 and seq >= SB_BAND and SB_CHUNK == LANES
    cast_in_specs, cast_out_specs, cast_shapes = [], [], []
    for w, layer in weights_to_cast:
        w_rows, w_cols = w.shape[-2:]
        block = w_rows // steps
        assert w_rows % steps == 0 and block % 16 == 0
        if w.ndim == 2:
            cast_in_specs.append(pl.BlockSpec((block, w_cols), lambda b, s: (b * slabs + s, 0)))
        else:
            cast_in_specs.append(pl.BlockSpec((None, block, w_cols), lambda b, s, layer=layer: (layer, b * slabs + s, 0)))
        cast_out_specs.append(pl.BlockSpec((block, w_cols), lambda b, s: (b * slabs + s, 0)))
        cast_shapes.append(jax.ShapeDtypeStruct((w_rows, w_cols), BF16))
    o, *cast = pl.pallas_call(
        functools.partial(_sb_kernel, n_cast=len(weights_to_cast)),
        out_shape=[jax.ShapeDtypeStruct((n, SB_HEADS * HEAD_DIM), BF16)] + cast_shapes,
        grid=(batch, slabs),
        in_specs=[
            pl.BlockSpec((seq, LANES), lambda b, s: (b, s)),
            pl.BlockSpec((seq, LANES), lambda b, s: (b, slabs + s)),
            pl.BlockSpec((seq, LANES), lambda b, s: (b, 2 * slabs + s)),
            _resident(tri.shape),
        ] + cast_in_specs,
        out_specs=[pl.BlockSpec((seq, LANES), lambda b, s: (b, s))] + cast_out_specs,
        scratch_shapes=[
            pltpu.VMEM((rows, SB_CHUNK), F32),
            pltpu.VMEM((2, SB_INTERLEAVE, rows, SB_BAND), F32),
            pltpu.VMEM((2, SB_INTERLEAVE, nck * rows, 2 * SB_CHUNK), BF16),
            pltpu.VMEM((2, SB_INTERLEAVE, rows, LANES), F32),
            pltpu.VMEM((2, SB_INTERLEAVE, rows, LANES), F32),
            pltpu.VMEM((rows, LANES), BF16),
            pltpu.VMEM((rows, LANES), F32),
            pltpu.VMEM((rows, LANES), F32),
        ],
        compiler_params=_params(("parallel", "parallel")),
        name="sb_attention",
    )(qkv, qkv, qkv, tri, *[w for w, _ in weights_to_cast])
    return o, cast


def _swa_kernel(sink_ref, bias_first_ref, bias_ref, q_ref, kvp_ref, kvc_ref, o_ref):
    w = WINDOW
    lane = lax.broadcasted_iota(jnp.int32, (w, LANES), 1)
    head0 = lane < HEAD_DIM
    zero = jnp.zeros((w, LANES), BF16)
    ones = jnp.ones((2 * w, LANES), BF16)
    for blk in range(SWA_BLOCKS_PER_STEP):
        rows = slice(blk * w, (blk + 1) * w)
        bias = jnp.concatenate([(bias_first_ref if blk == 0 else bias_ref)[0]] * SWA_GROUP, axis=0)
        for h in range(SWA_KV_HEADS):
            k_cols = slice(h * LANES, (h + 1) * LANES)
            v_cols = slice((SWA_KV_HEADS + h) * LANES, (SWA_KV_HEADS + h + 1) * LANES)
            if blk == 0:
                prev_k, prev_v = kvp_ref[:, k_cols], kvp_ref[:, v_cols]
            else:
                prev_k, prev_v = kvc_ref[(blk - 1) * w:blk * w, k_cols], kvc_ref[(blk - 1) * w:blk * w, v_cols]
            kk = jnp.concatenate([prev_k, kvc_ref[rows, k_cols]], axis=0)
            vv = jnp.concatenate([prev_v, kvc_ref[rows, v_cols]], axis=0)
            vv_ones = jnp.concatenate([vv, ones], axis=1)
            stacked, sinks = [], []
            for g in range(SWA_GROUP):
                head = h * SWA_GROUP + g
                slab = head // HEADS_PER_SLAB
                q = q_ref[rows, slab * LANES:(slab + 1) * LANES] * ATTN_SCALE
                stacked.append(jnp.where(head0, q, zero) if head % HEADS_PER_SLAB == 0 else jnp.where(head0, zero, q))
                sinks.append(jnp.full((w, LANES), sink_ref[head], F32))
            qs = jnp.concatenate(stacked, axis=0)
            sink = jnp.concatenate(sinks, axis=0)
            s = lax.dot_general(qs, kk, (((1,), (1,)), ((), ())), preferred_element_type=F32) + bias
            m = jnp.maximum(jnp.broadcast_to(jnp.max(s, axis=1, keepdims=True), sink.shape), sink)
            e = jnp.exp(s - jnp.concatenate([m, m], axis=1)).astype(BF16)
            ov = jnp.dot(e, vv_ones, preferred_element_type=F32)
            for pair in range(SWA_GROUP // HEADS_PER_SLAB):
                slab = (h * SWA_GROUP) // HEADS_PER_SLAB + pair
                lo, hi = slice(2 * pair * w, (2 * pair + 1) * w), slice((2 * pair + 1) * w, (2 * pair + 2) * w)
                both = lambda x: jnp.where(head0, x[lo], x[hi])
                denom = both(ov[:, LANES:]) + jnp.exp(both(sink) - both(m))
                o_ref[rows, slab * LANES:(slab + 1) * LANES] = (both(ov[:, :LANES]) * (1.0 / denom)).astype(o_ref.dtype)


def _swa_attention(q, kv, sinks, batch, seq):
    n, dq = q.shape
    step_rows = SWA_BLOCKS_PER_STEP * WINDOW
    steps = seq // step_rows
    dkv = kv.shape[1]
    qi = lax.broadcasted_iota(jnp.int32, (WINDOW, 2 * WINDOW), 0)
    ki = lax.broadcasted_iota(jnp.int32, (WINDOW, 2 * WINDOW), 1)
    diff = qi + WINDOW - ki
    in_window = (diff >= 0) & (diff < WINDOW)
    bias = jnp.where(jnp.stack([in_window & (ki >= WINDOW), in_window]), 0.0, -jnp.inf).astype(F32)
    bias_spec = lambda index: pl.BlockSpec((1, WINDOW, 2 * WINDOW), index)
    return pl.pallas_call(
        _swa_kernel,
        out_shape=jax.ShapeDtypeStruct((n, dq), BF16),
        grid=(batch, steps),
        in_specs=[
            pl.BlockSpec(memory_space=pltpu.SMEM),
            bias_spec(lambda b, i: (jnp.minimum(i, 1), 0, 0)),
            bias_spec(lambda b, i: (1, 0, 0)),
            pl.BlockSpec((step_rows, dq), lambda b, i: (b * steps + i, 0)),
            pl.BlockSpec((WINDOW, dkv),
                         lambda b, i: ((b * steps + i) * SWA_BLOCKS_PER_STEP - jnp.minimum(i, 1), 0)),
            pl.BlockSpec((step_rows, dkv), lambda b, i: (b * steps + i, 0)),
        ],
        out_specs=pl.BlockSpec((step_rows, dq), lambda b, i: (b * steps + i, 0)),
        compiler_params=_params(("parallel", "arbitrary")),
        name="swa_attention",
    )(sinks, bias, bias, q, kv, kv)


def kernel(x, ffn1_norm, ffn1_w_in, ffn1_w_out, mix_norm, ffn2_norm, ffn2_w_in, ffn2_w_out,
           sb_w_qkv, sb_w_o, kv_norm, kv_w, swa_w_q, swa_sinks, swa_w_o, final_norm):
    batch, seq, d = x.shape
    n = batch * seq
    cos, sin = _rotary_tables(seq)
    bf = _to_bf16
    h = x.reshape(n, d)

    h, w_qkv = _ffn(h, ffn1_norm[0], bf(ffn1_w_in, 0), bf(ffn1_w_out, 0), side_cast=(sb_w_qkv, 0), name="ffn1_l0")
    qkv = _proj(h, mix_norm[0], w_qkv, cos, sin, rot_slabs=0, name="proj_qkv")
    later_weights = [(ffn2_w_in, 0), (ffn2_w_out, 0), (sb_w_o, 0), (kv_w, None), (ffn1_w_in, 1), (ffn1_w_out, 1),
                     (swa_w_q, 0), (ffn2_w_in, 1), (ffn2_w_out, 1), (swa_w_o, 0)]
    o, (w2_in0, w2_out0, w_o0, w_kv, w1_in1, w1_out1, w_q1, w2_in1, w2_out1, w_o1) = _sb_attention(
        qkv, batch, seq, later_weights)
    h = _ffn(h, ffn2_norm[0], w2_in0, w2_out0, attn=(o, w_o0), name="ffn2_l0")
    k_slabs = SWA_KV_HEADS * HEAD_DIM // LANES
    kv = _proj(h, kv_norm, w_kv, cos, sin, rot_slabs=k_slabs, name="proj_kv", dup_heads=True)

    h = _ffn(h, ffn1_norm[1], w1_in1, w1_out1, name="ffn1_l1")
    q = _proj(h, mix_norm[1], w_q1, cos, sin, rot_slabs=SWA_Q_HEADS * HEAD_DIM // LANES, name="proj_q")
    o = _swa_attention(q, kv, swa_sinks[0], batch, seq)
    h = _ffn(h, ffn2_norm[1], w2_in1, w2_out1, attn=(o, w_o1), final_g=final_norm, name="ffn2_l1")
    return h.reshape(batch, seq, d)
```

```python
import functools

import jax
import jax.numpy as jnp
from jax import lax
from jax.experimental import pallas as pl
from jax.experimental.pallas import tpu as pltpu

F32 = jnp.float32
BF16 = jnp.bfloat16

D_MODEL = 1024
HEAD_DIM = 64
SB_HEADS = 16
SWA_Q_HEADS = 16
SWA_KV_HEADS = 4
SWA_GROUP = SWA_Q_HEADS // SWA_KV_HEADS
WINDOW = 128
D_FF = 2816
ROPE_THETA = 10000.0
RMS_EPS = 1e-6
FFN_RES_SCALE = 0.5
ATTN_SCALE = HEAD_DIM ** -0.5
LOG2E = 1.4426950408889634

LANES = 128
HEADS_PER_SLAB = LANES // HEAD_DIM
VMEM_LIMIT_BYTES = 56 * 1024 * 1024

TOKEN_TILE = 1024
PROJ_CHUNK = 256
CAST_BLOCK_BYTES = 4 * 1024 * 1024
FFN_TILE = 1024
FF_CHUNKS = 11
SWA_BLOCKS_PER_STEP = 4
SB_TILE = 64
SB_BAND = 256
SB_CHUNK = 128
SB_INTERLEAVE = 16
F32_EXP_UNDERFLOW = -104.0


def _rms(x, g):
    return x * lax.rsqrt(jnp.mean(x * x, axis=-1, keepdims=True) + RMS_EPS) * g


def _params(semantics):
    return pltpu.CompilerParams(dimension_semantics=semantics, vmem_limit_bytes=VMEM_LIMIT_BYTES)


def _resident(shape):
    return pl.BlockSpec(shape, lambda *_: (0,) * len(shape), pipeline_mode=pl.Buffered(1))


def _cast_kernel(w_ref, out_ref):
    out_ref[...] = w_ref[...].astype(out_ref.dtype)


def _to_bf16(w, layer=None, name="cast"):
    rows, cols = w.shape[-2:]
    block_rows = max(8, min(rows, CAST_BLOCK_BYTES // (4 * cols) // 8 * 8))
    while rows % block_rows:
        block_rows -= 8
    if layer is None:
        in_spec = pl.BlockSpec((block_rows, cols), lambda i: (i, 0))
    else:
        in_spec = pl.BlockSpec((None, block_rows, cols), lambda i: (layer, i, 0))
    return pl.pallas_call(
        _cast_kernel,
        out_shape=jax.ShapeDtypeStruct((rows, cols), BF16),
        grid=(rows // block_rows,),
        in_specs=[in_spec],
        out_specs=pl.BlockSpec((block_rows, cols), lambda i: (i, 0)),
        compiler_params=_params(("parallel",)),
        name=name,
    )(w)


def _ffn_kernel(*refs, has_attn, has_final_norm, has_side_cast):
    refs = list(refs)
    h_ref = refs.pop(0)
    if has_attn:
        o_ref, wo_ref = refs.pop(0), refs.pop(0)
    g_ref, win_ref, wout_ref = refs.pop(0), refs.pop(0), refs.pop(0)
    if has_final_norm:
        fg_ref = refs.pop(0)
    if has_side_cast:
        side_ref, out_ref, side_out_ref = refs
        side_out_ref[...] = side_ref[...].astype(side_out_ref.dtype)
    else:
        (out_ref,) = refs

    x = h_ref[...]
    if has_attn:
        x = x + jnp.dot(o_ref[...], wo_ref[...], preferred_element_type=F32)
    xn = _rms(x, g_ref[...]).astype(BF16)
    fc = D_FF // FF_CHUNKS
    y = None
    for c in range(FF_CHUNKS):
        gate = jnp.dot(xn, win_ref[:, c * fc:(c + 1) * fc], preferred_element_type=F32)
        up = jnp.dot(xn, win_ref[:, D_FF + c * fc:D_FF + (c + 1) * fc], preferred_element_type=F32)
        act = (gate * jax.nn.sigmoid(gate) * up).astype(BF16)
        yc = jnp.dot(act, wout_ref[c * fc:(c + 1) * fc, :], preferred_element_type=F32)
        y = yc if y is None else y + yc
    x = x + FFN_RES_SCALE * y
    if has_final_norm:
        x = _rms(x, fg_ref[...])
    out_ref[...] = x


def _ffn(h, g, w_in, w_out, attn=None, final_g=None, side_cast=None, name="ffn"):
    n, d = h.shape
    steps = n // FFN_TILE
    row = pl.BlockSpec((FFN_TILE, d), lambda i: (i, 0))
    args, specs = [h], [row]
    if attn is not None:
        o, w_o = attn
        args += [o, w_o]
        specs += [row, _resident(w_o.shape)]
    args += [g.reshape(1, d), w_in, w_out]
    specs += [_resident((1, d)), _resident(w_in.shape), _resident(w_out.shape)]
    if final_g is not None:
        args.append(final_g.reshape(1, d))
        specs.append(_resident((1, d)))
    out_shape, out_specs = jax.ShapeDtypeStruct((n, d), F32), row
    if side_cast is not None:
        w, layer = side_cast
        w_rows, w_cols = w.shape[-2:]
        block = w_rows // steps
        assert w_rows % steps == 0 and block % 16 == 0
        args.append(w)
        specs.append(pl.BlockSpec((None, block, w_cols), lambda i: (layer, i, 0)))
        out_shape = [out_shape, jax.ShapeDtypeStruct((w_rows, w_cols), BF16)]
        out_specs = [row, pl.BlockSpec((block, w_cols), lambda i: (i, 0))]
    return pl.pallas_call(
        functools.partial(_ffn_kernel, has_attn=attn is not None, has_final_norm=final_g is not None,
                          has_side_cast=side_cast is not None),
        out_shape=out_shape,
        grid=(steps,),
        in_specs=specs,
        out_specs=out_specs,
        compiler_params=_params(("parallel",)),
        name=name,
    )(*args)


def _proj_kernel(h_ref, g_ref, w_ref, cos_ref, sin_ref, out_ref, *, rot_slabs, dup_heads):
    xn = _rms(h_ref[...], g_ref[...]).astype(BF16)
    cos, sin = cos_ref[...], sin_ref[...]
    lane = lax.broadcasted_iota(jnp.int32, cos.shape, 1)
    first_half = (lane % HEAD_DIM) < (HEAD_DIM // 2)
    head0 = lane < HEAD_DIM
    slabs_per_chunk = PROJ_CHUNK // LANES
    for s in range(w_ref.shape[1] // LANES):
        if s % slabs_per_chunk == 0:
            y = jnp.dot(xn, w_ref[:, s * LANES:(s + slabs_per_chunk) * LANES], preferred_element_type=F32)
        ys = y[:, (s % slabs_per_chunk) * LANES:(s % slabs_per_chunk + 1) * LANES]
        if s < rot_slabs:
            partner = jnp.where(first_half,
                                pltpu.roll(ys, LANES - HEAD_DIM // 2, 1),
                                pltpu.roll(ys, HEAD_DIM // 2, 1))
            ys = ys * cos + partner * sin
        if dup_heads:
            swapped = pltpu.roll(ys, HEAD_DIM, 1)
            out_ref[:, 2 * s * LANES:(2 * s + 1) * LANES] = jnp.where(head0, ys, swapped).astype(out_ref.dtype)
            out_ref[:, (2 * s + 1) * LANES:(2 * s + 2) * LANES] = jnp.where(head0, swapped, ys).astype(out_ref.dtype)
        else:
            out_ref[:, s * LANES:(s + 1) * LANES] = ys.astype(out_ref.dtype)


def _proj(h, g, w, cos, sin, rot_slabs, name, dup_heads=False):
    n, d = h.shape
    m = w.shape[1] * (HEADS_PER_SLAB if dup_heads else 1)
    tiles_per_seq = cos.shape[0] // TOKEN_TILE
    table = pl.BlockSpec((TOKEN_TILE, LANES), lambda i: (i % tiles_per_seq, 0))
    return pl.pallas_call(
        functools.partial(_proj_kernel, rot_slabs=rot_slabs, dup_heads=dup_heads),
        out_shape=jax.ShapeDtypeStruct((n, m), BF16),
        grid=(n // TOKEN_TILE,),
        in_specs=[pl.BlockSpec((TOKEN_TILE, d), lambda i: (i, 0)), _resident((1, d)), _resident(w.shape),
                  table, table],
        out_specs=pl.BlockSpec((TOKEN_TILE, m), lambda i: (i, 0)),
        compiler_params=_params(("parallel",)),
        name=name,
    )(h, g.reshape(1, d), w, cos, sin)


def _rotary_tables(seq):
    half = HEAD_DIM // 2
    inv_freq = ROPE_THETA ** (-jnp.arange(half, dtype=F32) / half)
    ang = jnp.arange(seq, dtype=F32)[:, None] * inv_freq[None, :]
    cos, sin = jnp.cos(ang), jnp.sin(ang)
    cos_head = jnp.concatenate([cos, cos], axis=-1)
    sin_head = jnp.concatenate([-sin, sin], axis=-1)
    return jnp.tile(cos_head, (1, HEADS_PER_SLAB)), jnp.tile(sin_head, (1, HEADS_PER_SLAB))


def _sb_kernel(*refs, n_cast):
    q_ref, k_ref, v_ref, tri_ref = refs[:4]
    cast_in = refs[4:4 + n_cast]
    o_ref = refs[4 + n_cast]
    cast_out = refs[5 + n_cast:5 + 2 * n_cast]
    mask_ref, lb_ref, split_ref, acc_ref, carry_ref, gqs_ref, gacc_ref, gcarry_ref = refs[5 + 2 * n_cast:]
    for w_ref, w_out_ref in zip(cast_in, cast_out):
        w_out_ref[...] = w_ref[...].astype(w_out_ref.dtype)

    t, band, ck, u_tiles = SB_TILE, SB_BAND, SB_CHUNK, SB_INTERLEAVE
    nck = band // ck
    seq = q_ref.shape[0]
    q_tiles = seq // t
    head_tiles = band // t
    band_tiles = q_tiles - head_tiles
    sizes = [u_tiles] * (band_tiles // u_tiles) + ([band_tiles % u_tiles] if band_tiles % u_tiles else [])
    lane = lax.broadcasted_iota(jnp.int32, (t, LANES), 1)
    head0 = lane < HEAD_DIM
    qrow = lax.broadcasted_iota(jnp.int32, (2 * t, ck), 0) & (t - 1)
    col = lax.broadcasted_iota(jnp.int32, (2 * t, ck), 1)
    mask_ref[...] = jnp.where(col < qrow + (band - t - (nck - 1) * ck), 1.0, 0.0)

    def aligned(offset):
        return offset if isinstance(offset, int) else pl.multiple_of(offset, t)

    def stacked_queries(i):
        q = q_ref[pl.ds(aligned(i * t), t), :] * ATTN_SCALE
        zero = jnp.zeros_like(q)
        return jnp.concatenate([jnp.where(head0, q, zero), jnp.where(head0, zero, q)], axis=0)

    def key_rows(start):
        return pl.ds(aligned(start), band)

    def logits(qs, start):
        z = lax.dot_general(qs, k_ref[key_rows(start), :], (((1,), (1,)), ((), ())),
                            preferred_element_type=F32)
        sp = jnp.log(1.0 + jnp.exp2(jnp.abs(z) * -LOG2E))
        log_beta = jnp.minimum(z, 0.0) - sp
        return log_beta, log_beta - z

    def split_chunk(l):
        hi = l.astype(BF16)
        lo = (l - hi.astype(F32)).astype(BF16)
        return jnp.concatenate([hi, lo], axis=1)

    def suffix_sums(split):
        return jnp.dot(split, tri_ref[...], preferred_element_type=F32)

    def chunk(x, c):
        return x[:, c * ck:(c + 1) * ck]

    def write_output(i, acc):
        o_ref[pl.ds(aligned(i * t), t), :] = jnp.where(head0, acc[:t], acc[t:]).astype(o_ref.dtype)

    def band_start(i):
        return (i - (head_tiles - 1)) * t

    def logits_stage(parity, slot, i):
        log_beta, log_1m = logits(stacked_queries(i), band_start(i))
        lb_ref[parity, slot] = log_beta
        for c in range(nck):
            l = chunk(log_1m, c)
            if c == nck - 1:
                l = l * mask_ref[...]
            split_ref[parity, slot, c * 2 * t:(c + 1) * 2 * t, :] = split_chunk(l)

    def weights_stage(parity, slot, i, sums):
        log_beta = lb_ref[parity, slot]
        later = None
        ws = [None] * nck
        for c in reversed(range(nck)):
            sums_c = sums[c * 2 * t:(c + 1) * 2 * t]
            x = chunk(log_beta, c) + chunk(sums_c, 0)
            if later is not None:
                x = x + later
            w = jnp.exp(x)
            if c == nck - 1:
                w = w * mask_ref[...]
            ws[c] = w.astype(BF16)
            later = chunk(sums_c, 1) if later is None else later + chunk(sums_c, 1)
        acc = jnp.dot(jnp.concatenate(ws, axis=1), v_ref[key_rows(band_start(i)), :],
                      preferred_element_type=F32)
        acc_ref[parity, slot] = acc
        carry_ref[parity, slot] = later
        write_output(i, acc)
        return later

    def generic_sweep(i, start, limit):
        log_beta, log_1m = logits(gqs_ref[...], start)
        bound = jnp.minimum(i * t + qrow, limit) - start
        valids = [col + c * ck < bound for c in range(nck)]
        sums = suffix_sums(jnp.concatenate(
            [split_chunk(jnp.where(valids[c], chunk(log_1m, c), 0.0)) for c in range(nck)], axis=0))
        later = gcarry_ref[...]
        ws = [None] * nck
        for c in reversed(range(nck)):
            sums_c = sums[c * 2 * t:(c + 1) * 2 * t]
            x = chunk(log_beta, c) + chunk(sums_c, 0) + later
            ws[c] = jnp.where(valids[c], jnp.exp(x), 0.0).astype(BF16)
            later = later + chunk(sums_c, 1)
        gacc_ref[...] += jnp.dot(jnp.concatenate(ws, axis=1), v_ref[key_rows(start), :],
                                 preferred_element_type=F32)
        gcarry_ref[...] = later
        return jnp.max(later)

    def head_tile(i, c):
        gqs_ref[...] = stacked_queries(i)
        gacc_ref[...] = jnp.zeros_like(gacc_ref)
        gcarry_ref[...] = jnp.zeros_like(gcarry_ref)
        generic_sweep(i, 0, seq)
        write_output(i, gacc_ref[...])
        return c

    def continue_tile(parity, slot, i):
        gqs_ref[...] = stacked_queries(i)
        gacc_ref[...] = acc_ref[parity, slot]
        gcarry_ref[...] = carry_ref[parity, slot]

        def cond(state):
            limit, top = state
            return jnp.logical_and(limit > 0, top > F32_EXP_UNDERFLOW)

        def body(state):
            limit, _ = state
            start = jnp.maximum(limit - band, 0)
            return start, generic_sweep(i, start, limit)

        lax.while_loop(cond, body, (band_start(i), jnp.max(gcarry_ref[...])))
        write_output(i, gacc_ref[...])

    def first_tile(g):
        return head_tiles + (sum(sizes[:g]) if isinstance(g, int) else g * u_tiles)

    def stage(parity, group, size, next_group, next_size):
        sums = [suffix_sums(split_ref[parity, slot]) for slot in range(size)]
        if next_group is not None:
            for slot in range(next_size):
                logits_stage(1 - parity, slot, first_tile(next_group) + slot)
        top = None
        for slot in range(size):
            carry = weights_stage(parity, slot, first_tile(group) + slot, sums[slot])
            top = carry if top is None else jnp.maximum(top, carry)

        @pl.when(jnp.max(top) > F32_EXP_UNDERFLOW)
        def _():
            def one(slot, c):
                continue_tile(parity, slot, first_tile(group) + slot)
                return c
            lax.fori_loop(0, size, one, 0)

    lax.fori_loop(0, head_tiles, head_tile, 0)
    for slot in range(sizes[0]):
        logits_stage(0, slot, first_tile(0) + slot)

    def stage_pair(m, c):
        stage(0, 2 * m, u_tiles, 2 * m + 1, u_tiles)
        stage(1, 2 * m + 1, u_tiles, 2 * m + 2, u_tiles)
        return c

    full_groups = band_tiles // u_tiles
    pairs = max(full_groups - 1, 0) // 2
    lax.fori_loop(0, pairs, stage_pair, 0)
    for g in range(2 * pairs, len(sizes)):
        has_next = g + 1 < len(sizes)
        stage(g % 2, g, sizes[g], g + 1 if has_next else None, sizes[g + 1] if has_next else 0)


def _suffix_matrix(n):
    row = lax.broadcasted_iota(jnp.int32, (n, n), 0)
    col = lax.broadcasted_iota(jnp.int32, (n, n), 1)
    half = jnp.concatenate([(row > col).astype(BF16), jnp.ones((n, n), BF16)], axis=1)
    return jnp.concatenate([half, half], axis=0)


def _sb_attention(qkv, batch, seq, weights_to_cast):
    n = qkv.shape[0]
    slabs = SB_HEADS // HEADS_PER_SLAB
    steps = batch * slabs
    tri = _suffix_matrix(SB_CHUNK)
    rows, nck = 2 * SB_TILE, SB_BAND // SB_CHUNK
    assert seq % SB_TILE == 0 and seq >= SB_BAND and SB_CHUNK == LANES
    cast_in_specs, cast_out_specs, cast_shapes = [], [], []
    for w, layer in weights_to_cast:
        w_rows, w_cols = w.shape[-2:]
        block = w_rows // steps
        assert w_rows % steps == 0 and block % 16 == 0
        if w.ndim == 2:
            cast_in_specs.append(pl.BlockSpec((block, w_cols), lambda b, s: (b * slabs + s, 0)))
        else:
            cast_in_specs.append(pl.BlockSpec((None, block, w_cols), lambda b, s, layer=layer: (layer, b * slabs + s, 0)))
        cast_out_specs.append(pl.BlockSpec((block, w_cols), lambda b, s: (b * slabs + s, 0)))
        cast_shapes.append(jax.ShapeDtypeStruct((w_rows, w_cols), BF16))
    o, *cast = pl.pallas_call(
        functools.partial(_sb_kernel, n_cast=len(weights_to_cast)),
        out_shape=[jax.ShapeDtypeStruct((n, SB_HEADS * HEAD_DIM), BF16)] + cast_shapes,
        grid=(batch, slabs),
        in_specs=[
            pl.BlockSpec((seq, LANES), lambda b, s: (b, s)),
            pl.BlockSpec((seq, LANES), lambda b, s: (b, slabs + s)),
            pl.BlockSpec((seq, LANES), lambda b, s: (b, 2 * slabs + s)),
            _resident(tri.shape),
        ] + cast_in_specs,
        out_specs=[pl.BlockSpec((seq, LANES), lambda b, s: (b, s))] + cast_out_specs,
        scratch_shapes=[
            pltpu.VMEM((rows, SB_CHUNK), F32),
            pltpu.VMEM((2, SB_INTERLEAVE, rows, SB_BAND), F32),
            pltpu.VMEM((2, SB_INTERLEAVE, nck * rows, 2 * SB_CHUNK), BF16),
            pltpu.VMEM((2, SB_INTERLEAVE, rows, LANES), F32),
            pltpu.VMEM((2, SB_INTERLEAVE, rows, LANES), F32),
            pltpu.VMEM((rows, LANES), BF16),
            pltpu.VMEM((rows, LANES), F32),
            pltpu.VMEM((rows, LANES), F32),
        ],
        compiler_params=_params(("parallel", "parallel")),
        name="sb_attention",
    )(qkv, qkv, qkv, tri, *[w for w, _ in weights_to_cast])
    return o, cast


def _swa_kernel(sink_ref, bias_first_ref, bias_ref, q_ref, kvp_ref, kvc_ref, o_ref):
    w = WINDOW
    lane = lax.broadcasted_iota(jnp.int32, (w, LANES), 1)
    head0 = lane < HEAD_DIM
    zero = jnp.zeros((w, LANES), BF16)
    ones = jnp.ones((2 * w, LANES), BF16)
    for blk in range(SWA_BLOCKS_PER_STEP):
        rows = slice(blk * w, (blk + 1) * w)
        bias = jnp.concatenate([(bias_first_ref if blk == 0 else bias_ref)[0]] * SWA_GROUP, axis=0)
        for h in range(SWA_KV_HEADS):
            k_cols = slice(h * LANES, (h + 1) * LANES)
            v_cols = slice((SWA_KV_HEADS + h) * LANES, (SWA_KV_HEADS + h + 1) * LANES)
            if blk == 0:
                prev_k, prev_v = kvp_ref[:, k_cols], kvp_ref[:, v_cols]
            else:
                prev_k, prev_v = kvc_ref[(blk - 1) * w:blk * w, k_cols], kvc_ref[(blk - 1) * w:blk * w, v_cols]
            kk = jnp.concatenate([prev_k, kvc_ref[rows, k_cols]], axis=0)
            vv = jnp.concatenate([prev_v, kvc_ref[rows, v_cols]], axis=0)
            vv_ones = jnp.concatenate([vv, ones], axis=1)
            stacked, sinks = [], []
            for g in range(SWA_GROUP):
                head = h * SWA_GROUP + g
                slab = head // HEADS_PER_SLAB
                q = q_ref[rows, slab * LANES:(slab + 1) * LANES] * ATTN_SCALE
                stacked.append(jnp.where(head0, q, zero) if head % HEADS_PER_SLAB == 0 else jnp.where(head0, zero, q))
                sinks.append(jnp.full((w, LANES), sink_ref[head], F32))
            qs = jnp.concatenate(stacked, axis=0)
            sink = jnp.concatenate(sinks, axis=0)
            s = lax.dot_general(qs, kk, (((1,), (1,)), ((), ())), preferred_element_type=F32) + bias
            m = jnp.maximum(jnp.broadcast_to(jnp.max(s, axis=1, keepdims=True), sink.shape), sink)
            e = jnp.exp(s - jnp.concatenate([m, m], axis=1)).astype(BF16)
            ov = jnp.dot(e, vv_ones, preferred_element_type=F32)
            for pair in range(SWA_GROUP // HEADS_PER_SLAB):
                slab = (h * SWA_GROUP) // HEADS_PER_SLAB + pair
                lo, hi = slice(2 * pair * w, (2 * pair + 1) * w), slice((2 * pair + 1) * w, (2 * pair + 2) * w)
                both = lambda x: jnp.where(head0, x[lo], x[hi])
                denom = both(ov[:, LANES:]) + jnp.exp(both(sink) - both(m))
                o_ref[rows, slab * LANES:(slab + 1) * LANES] = (both(ov[:, :LANES]) * (1.0 / denom)).astype(o_ref.dtype)


def _swa_attention(q, kv, sinks, batch, seq):
    n, dq = q.shape
    step_rows = SWA_BLOCKS_PER_STEP * WINDOW
    steps = seq // step_rows
    dkv = kv.shape[1]
    qi = lax.broadcasted_iota(jnp.int32, (WINDOW, 2 * WINDOW), 0)
    ki = lax.broadcasted_iota(jnp.int32, (WINDOW, 2 * WINDOW), 1)
    diff = qi + WINDOW - ki
    in_window = (diff >= 0) & (diff < WINDOW)
    bias = jnp.where(jnp.stack([in_window & (ki >= WINDOW), in_window]), 0.0, -jnp.inf).astype(F32)
    bias_spec = lambda index: pl.BlockSpec((1, WINDOW, 2 * WINDOW), index)
    return pl.pallas_call(
        _swa_kernel,
        out_shape=jax.ShapeDtypeStruct((n, dq), BF16),
        grid=(batch, steps),
        in_specs=[
            pl.BlockSpec(memory_space=pltpu.SMEM),
            bias_spec(lambda b, i: (jnp.minimum(i, 1), 0, 0)),
            bias_spec(lambda b, i: (1, 0, 0)),
            pl.BlockSpec((step_rows, dq), lambda b, i: (b * steps + i, 0)),
            pl.BlockSpec((WINDOW, dkv),
                         lambda b, i: ((b * steps + i) * SWA_BLOCKS_PER_STEP - jnp.minimum(i, 1), 0)),
            pl.BlockSpec((step_rows, dkv), lambda b, i: (b * steps + i, 0)),
        ],
        out_specs=pl.BlockSpec((step_rows, dq), lambda b, i: (b * steps + i, 0)),
        compiler_params=_params(("parallel", "arbitrary")),
        name="swa_attention",
    )(sinks, bias, bias, q, kv, kv)


def kernel(x, ffn1_norm, ffn1_w_in, ffn1_w_out, mix_norm, ffn2_norm, ffn2_w_in, ffn2_w_out,
           sb_w_qkv, sb_w_o, kv_norm, kv_w, swa_w_q, swa_sinks, swa_w_o, final_norm):
    batch, seq, d = x.shape
    n = batch * seq
    cos, sin = _rotary_tables(seq)
    bf = _to_bf16
    h = x.reshape(n, d)

    h, w_qkv = _ffn(h, ffn1_norm[0], bf(ffn1_w_in, 0), bf(ffn1_w_out, 0), side_cast=(sb_w_qkv, 0), name="ffn1_l0")
    qkv = _proj(h, mix_norm[0], w_qkv, cos, sin, rot_slabs=0, name="proj_qkv")
    later_weights = [(ffn2_w_in, 0), (ffn2_w_out, 0), (sb_w_o, 0), (kv_w, None), (ffn1_w_in, 1), (ffn1_w_out, 1),
                     (swa_w_q, 0), (ffn2_w_in, 1), (ffn2_w_out, 1), (swa_w_o, 0)]
    o, (w2_in0, w2_out0, w_o0, w_kv, w1_in1, w1_out1, w_q1, w2_in1, w2_out1, w_o1) = _sb_attention(
        qkv, batch, seq, later_weights)
    h = _ffn(h, ffn2_norm[0], w2_in0, w2_out0, attn=(o, w_o0), name="ffn2_l0")
    k_slabs = SWA_KV_HEADS * HEAD_DIM // LANES
    kv = _proj(h, kv_norm, w_kv, cos, sin, rot_slabs=k_slabs, name="proj_kv", dup_heads=True)

    h = _ffn(h, ffn1_norm[1], w1_in1, w1_out1, name="ffn1_l1")
    q = _proj(h, mix_norm[1], w_q1, cos, sin, rot_slabs=SWA_Q_HEADS * HEAD_DIM // LANES, name="proj_q")
    o = _swa_attention(q, kv, swa_sinks[0], batch, seq)
    h = _ffn(h, ffn2_norm[1], w2_in1, w2_out1, attn=(o, w_o1), final_g=final_norm, name="ffn2_l1")
    return h.reshape(batch, seq, d)
```

```python
import functools

import jax
import jax.numpy as jnp
from jax import lax
from jax.experimental import pallas as pl
from jax.experimental.pallas import tpu as pltpu

F32 = jnp.float32
BF16 = jnp.bfloat16

HEAD_DIM = 64
SB_HEADS = 16
SWA_Q_HEADS = 16
SWA_KV_HEADS = 4
SWA_GROUP = SWA_Q_HEADS // SWA_KV_HEADS
WINDOW = 128
D_FF = 2816
ROPE_THETA = 10000.0
RMS_EPS = 1e-6
FFN_RES_SCALE = 0.5
ATTN_SCALE = HEAD_DIM ** -0.5
LOG2E = 1.4426950408889634

LANES = 128
BF16_SUBLANES = 16
HEADS_PER_SLAB = LANES // HEAD_DIM
VMEM_LIMIT_BYTES = 56 * 1024 * 1024

TOKEN_TILE = 1024
PROJ_CHUNK = 256
CAST_BLOCK_BYTES = 4 * 1024 * 1024
FFN_TILE = 1024
FF_CHUNKS = 11
SWA_BLOCKS_PER_STEP = 8
SB_TILE = 64
SB_BAND = 256
SB_CHUNK = 128
SB_INTERLEAVE = 16
F32_EXP_UNDERFLOW = -105.0


def _rms(x, g):
    return x * lax.rsqrt(jnp.mean(x * x, axis=-1, keepdims=True) + RMS_EPS) * g


def _params(semantics):
    return pltpu.CompilerParams(dimension_semantics=semantics, vmem_limit_bytes=VMEM_LIMIT_BYTES)


def _resident(shape):
    return pl.BlockSpec(shape, lambda *_: (0,) * len(shape), pipeline_mode=pl.Buffered(1))


def _cast_kernel(w_ref, out_ref):
    out_ref[...] = w_ref[...].astype(out_ref.dtype)


def _to_bf16(w, layer=None, name="cast"):
    rows, cols = w.shape[-2:]
    block_rows = max(BF16_SUBLANES, min(rows, CAST_BLOCK_BYTES // (4 * cols) // BF16_SUBLANES * BF16_SUBLANES))
    while rows % block_rows:
        block_rows -= BF16_SUBLANES
    if layer is None:
        in_spec = pl.BlockSpec((block_rows, cols), lambda i: (i, 0))
    else:
        in_spec = pl.BlockSpec((None, block_rows, cols), lambda i: (layer, i, 0))
    return pl.pallas_call(
        _cast_kernel,
        out_shape=jax.ShapeDtypeStruct((rows, cols), BF16),
        grid=(rows // block_rows,),
        in_specs=[in_spec],
        out_specs=pl.BlockSpec((block_rows, cols), lambda i: (i, 0)),
        compiler_params=_params(("parallel",)),
        name=name,
    )(w)


def _ffn_kernel(*refs, has_attn, has_final_norm, has_side_cast):
    refs = list(refs)
    h_ref = refs.pop(0)
    if has_attn:
        o_ref, wo_ref = refs.pop(0), refs.pop(0)
    g_ref, win_ref, wout_ref = refs.pop(0), refs.pop(0), refs.pop(0)
    if has_final_norm:
        fg_ref = refs.pop(0)
    if has_side_cast:
        side_ref, out_ref, side_out_ref = refs
        side_out_ref[...] = side_ref[...].astype(side_out_ref.dtype)
    else:
        (out_ref,) = refs

    x = h_ref[...]
    if has_attn:
        x = x + jnp.dot(o_ref[...], wo_ref[...], preferred_element_type=F32)
    xn = _rms(x, g_ref[...]).astype(BF16)
    fc = D_FF // FF_CHUNKS
    y = None
    for c in range(FF_CHUNKS):
        gate = jnp.dot(xn, win_ref[:, c * fc:(c + 1) * fc], preferred_element_type=F32)
        up = jnp.dot(xn, win_ref[:, D_FF + c * fc:D_FF + (c + 1) * fc], preferred_element_type=F32)
        act = (gate * jax.nn.sigmoid(gate) * up).astype(BF16)
        yc = jnp.dot(act, wout_ref[c * fc:(c + 1) * fc, :], preferred_element_type=F32)
        y = yc if y is None else y + yc
    x = x + FFN_RES_SCALE * y
    if has_final_norm:
        x = _rms(x, fg_ref[...])
    out_ref[...] = x


def _ffn(h, g, w_in, w_out, attn=None, final_g=None, side_cast=None, name="ffn"):
    n, d = h.shape
    steps = n // FFN_TILE
    row = pl.BlockSpec((FFN_TILE, d), lambda i: (i, 0))
    args, specs = [h], [row]
    if attn is not None:
        o, w_o = attn
        args += [o, w_o]
        specs += [row, _resident(w_o.shape)]
    args += [g.reshape(1, d), w_in, w_out]
    specs += [_resident((1, d)), _resident(w_in.shape), _resident(w_out.shape)]
    if final_g is not None:
        args.append(final_g.reshape(1, d))
        specs.append(_resident((1, d)))
    out_shape, out_specs = jax.ShapeDtypeStruct((n, d), F32), row
    if side_cast is not None:
        w, layer = side_cast
        w_rows, w_cols = w.shape[-2:]
        block = w_rows // steps
        assert w_rows % steps == 0 and block % BF16_SUBLANES == 0
        args.append(w)
        specs.append(pl.BlockSpec((None, block, w_cols), lambda i: (layer, i, 0)))
        out_shape = [out_shape, jax.ShapeDtypeStruct((w_rows, w_cols), BF16)]
        out_specs = [row, pl.BlockSpec((block, w_cols), lambda i: (i, 0))]
    return pl.pallas_call(
        functools.partial(_ffn_kernel, has_attn=attn is not None, has_final_norm=final_g is not None,
                          has_side_cast=side_cast is not None),
        out_shape=out_shape,
        grid=(steps,),
        in_specs=specs,
        out_specs=out_specs,
        compiler_params=_params(("parallel",)),
        name=name,
    )(*args)


def _proj_kernel(h_ref, g_ref, w_ref, cos_ref, sin_ref, out_ref, *, rot_slabs, dup_heads):
    xn = _rms(h_ref[...], g_ref[...]).astype(BF16)
    cos, sin = cos_ref[...], sin_ref[...]
    lane = lax.broadcasted_iota(jnp.int32, cos.shape, 1)
    first_half = (lane % HEAD_DIM) < (HEAD_DIM // 2)
    head0 = lane < HEAD_DIM
    slabs_per_chunk = PROJ_CHUNK // LANES
    for s in range(w_ref.shape[1] // LANES):
        if s % slabs_per_chunk == 0:
            y = jnp.dot(xn, w_ref[:, s * LANES:(s + slabs_per_chunk) * LANES], preferred_element_type=F32)
        ys = y[:, (s % slabs_per_chunk) * LANES:(s % slabs_per_chunk + 1) * LANES]
        if s < rot_slabs:
            partner = jnp.where(first_half,
                                pltpu.roll(ys, LANES - HEAD_DIM // 2, 1),
                                pltpu.roll(ys, HEAD_DIM // 2, 1))
            ys = ys * cos + partner * sin
        if dup_heads:
            swapped = pltpu.roll(ys, HEAD_DIM, 1)
            out_ref[:, 2 * s * LANES:(2 * s + 1) * LANES] = jnp.where(head0, ys, swapped).astype(out_ref.dtype)
            out_ref[:, (2 * s + 1) * LANES:(2 * s + 2) * LANES] = jnp.where(head0, swapped, ys).astype(out_ref.dtype)
        else:
            out_ref[:, s * LANES:(s + 1) * LANES] = ys.astype(out_ref.dtype)


def _proj(h, g, w, cos, sin, rot_slabs, name, dup_heads=False):
    n, d = h.shape
    m = w.shape[1] * (HEADS_PER_SLAB if dup_heads else 1)
    tiles_per_seq = cos.shape[0] // TOKEN_TILE
    table = pl.BlockSpec((TOKEN_TILE, LANES), lambda i: (i % tiles_per_seq, 0))
    return pl.pallas_call(
        functools.partial(_proj_kernel, rot_slabs=rot_slabs, dup_heads=dup_heads),
        out_shape=jax.ShapeDtypeStruct((n, m), BF16),
        grid=(n // TOKEN_TILE,),
        in_specs=[pl.BlockSpec((TOKEN_TILE, d), lambda i: (i, 0)), _resident((1, d)), _resident(w.shape),
                  table, table],
        out_specs=pl.BlockSpec((TOKEN_TILE, m), lambda i: (i, 0)),
        compiler_params=_params(("parallel",)),
        name=name,
    )(h, g.reshape(1, d), w, cos, sin)


def _rotary_tables(seq):
    half = HEAD_DIM // 2
    inv_freq = ROPE_THETA ** (-jnp.arange(half, dtype=F32) / half)
    ang = jnp.arange(seq, dtype=F32)[:, None] * inv_freq[None, :]
    cos, sin = jnp.cos(ang), jnp.sin(ang)
    cos_head = jnp.concatenate([cos, cos], axis=-1)
    sin_head = jnp.concatenate([-sin, sin], axis=-1)
    return jnp.tile(cos_head, (1, HEADS_PER_SLAB)), jnp.tile(sin_head, (1, HEADS_PER_SLAB))


def _sb_kernel(*refs, n_cast):
    q_ref, k_ref, v_ref, tri_ref = refs[:4]
    cast_in = refs[4:4 + n_cast]
    o_ref = refs[4 + n_cast]
    cast_out = refs[5 + n_cast:5 + 2 * n_cast]
    mask_ref, lb_ref, split_ref, acc_ref, carry_ref, gqs_ref, gacc_ref, gcarry_ref = refs[5 + 2 * n_cast:]
    for w_ref, w_out_ref in zip(cast_in, cast_out):
        w_out_ref[...] = w_ref[...].astype(w_out_ref.dtype)

    t, band, ck, u_tiles = SB_TILE, SB_BAND, SB_CHUNK, SB_INTERLEAVE
    nck = band // ck
    seq = q_ref.shape[0]
    q_tiles = seq // t
    head_tiles = band // t
    band_tiles = q_tiles - head_tiles
    sizes = [u_tiles] * (band_tiles // u_tiles) + ([band_tiles % u_tiles] if band_tiles % u_tiles else [])
    lane = lax.broadcasted_iota(jnp.int32, (t, LANES), 1)
    head0 = lane < HEAD_DIM
    qrow = lax.broadcasted_iota(jnp.int32, (2 * t, ck), 0) & (t - 1)
    col = lax.broadcasted_iota(jnp.int32, (2 * t, ck), 1)
    mask_ref[...] = jnp.where(col < qrow + (band - t - (nck - 1) * ck), 1.0, 0.0)

    def aligned(offset):
        return offset if isinstance(offset, int) else pl.multiple_of(offset, t)

    def stacked_queries(i):
        q = q_ref[pl.ds(aligned(i * t), t), :] * ATTN_SCALE
        zero = jnp.zeros_like(q)
        return jnp.concatenate([jnp.where(head0, q, zero), jnp.where(head0, zero, q)], axis=0)

    def key_rows(start):
        return pl.ds(aligned(start), band)

    def logits(qs, start):
        z = lax.dot_general(qs, k_ref[key_rows(start), :], (((1,), (1,)), ((), ())),
                            preferred_element_type=F32)
        sp = jnp.log(1.0 + jnp.exp2(jnp.abs(z) * -LOG2E))
        log_beta = jnp.minimum(z, 0.0) - sp
        return log_beta, log_beta - z

    def split_chunk(l):
        hi = l.astype(BF16)
        lo = (l - hi.astype(F32)).astype(BF16)
        return jnp.concatenate([hi, lo], axis=1)

    def suffix_sums(split):
        return jnp.dot(split, tri_ref[...], preferred_element_type=F32)

    def chunk(x, c):
        return x[:, c * ck:(c + 1) * ck]

    def write_output(i, acc):
        o_ref[pl.ds(aligned(i * t), t), :] = jnp.where(head0, acc[:t], acc[t:]).astype(o_ref.dtype)

    def band_start(i):
        return (i - (head_tiles - 1)) * t

    def logits_stage(parity, slot, i):
        log_beta, log_1m = logits(stacked_queries(i), band_start(i))
        lb_ref[parity, slot] = log_beta
        for c in range(nck):
            l = chunk(log_1m, c)
            if c == nck - 1:
                l = l * mask_ref[...]
            split_ref[parity, slot, c * 2 * t:(c + 1) * 2 * t, :] = split_chunk(l)

    def weights_stage(parity, slot, i, sums):
        log_beta = lb_ref[parity, slot]
        later = None
        ws = [None] * nck
        for c in reversed(range(nck)):
            sums_c = sums[c * 2 * t:(c + 1) * 2 * t]
            x = chunk(log_beta, c) + chunk(sums_c, 0)
            if later is not None:
                x = x + later
            w = jnp.exp(x)
            if c == nck - 1:
                w = w * mask_ref[...]
            ws[c] = w.astype(BF16)
            later = chunk(sums_c, 1) if later is None else later + chunk(sums_c, 1)
        acc = jnp.dot(jnp.concatenate(ws, axis=1), v_ref[key_rows(band_start(i)), :],
                      preferred_element_type=F32)
        acc_ref[parity, slot] = acc
        carry_ref[parity, slot] = later
        write_output(i, acc)
        return later

    def generic_sweep(i, start, limit):
        log_beta, log_1m = logits(gqs_ref[...], start)
        bound = jnp.minimum(i * t + qrow, limit) - start
        valids = [col + c * ck < bound for c in range(nck)]
        sums = suffix_sums(jnp.concatenate(
            [split_chunk(jnp.where(valids[c], chunk(log_1m, c), 0.0)) for c in range(nck)], axis=0))
        later = gcarry_ref[...]
        ws = [None] * nck
        for c in reversed(range(nck)):
            sums_c = sums[c * 2 * t:(c + 1) * 2 * t]
            x = chunk(log_beta, c) + chunk(sums_c, 0) + later
            ws[c] = jnp.where(valids[c], jnp.exp(x), 0.0).astype(BF16)
            later = later + chunk(sums_c, 1)
        gacc_ref[...] += jnp.dot(jnp.concatenate(ws, axis=1), v_ref[key_rows(start), :],
                                 preferred_element_type=F32)
        gcarry_ref[...] = later
        return jnp.max(later)

    def head_tile(i, c):
        gqs_ref[...] = stacked_queries(i)
        gacc_ref[...] = jnp.zeros_like(gacc_ref)
        gcarry_ref[...] = jnp.zeros_like(gcarry_ref)
        generic_sweep(i, 0, seq)
        write_output(i, gacc_ref[...])
        return c

    def continue_tile(parity, slot, i):
        gqs_ref[...] = stacked_queries(i)
        gacc_ref[...] = acc_ref[parity, slot]
        gcarry_ref[...] = carry_ref[parity, slot]

        def cond(state):
            limit, top = state
            return jnp.logical_and(limit > 0, top > F32_EXP_UNDERFLOW)

        def body(state):
            limit, _ = state
            start = jnp.maximum(limit - band, 0)
            return start, generic_sweep(i, start, limit)

        lax.while_loop(cond, body, (band_start(i), jnp.max(gcarry_ref[...])))
        write_output(i, gacc_ref[...])

    def first_tile(g):
        return head_tiles + (sum(sizes[:g]) if isinstance(g, int) else g * u_tiles)

    def stage(parity, group, size, next_group, next_size):
        sums = [suffix_sums(split_ref[parity, slot]) for slot in range(size)]
        if next_group is not None:
            for slot in range(next_size):
                logits_stage(1 - parity, slot, first_tile(next_group) + slot)
        top = None
        for slot in range(size):
            carry = weights_stage(parity, slot, first_tile(group) + slot, sums[slot])
            top = carry if top is None else jnp.maximum(top, carry)

        @pl.when(jnp.max(top) > F32_EXP_UNDERFLOW)
        def _():
            def one(slot, c):
                continue_tile(parity, slot, first_tile(group) + slot)
                return c
            lax.fori_loop(0, size, one, 0)

    lax.fori_loop(0, head_tiles, head_tile, 0)
    for slot in range(sizes[0]):
        logits_stage(0, slot, first_tile(0) + slot)

    def stage_pair(m, c):
        stage(0, 2 * m, u_tiles, 2 * m + 1, u_tiles)
        stage(1, 2 * m + 1, u_tiles, 2 * m + 2, u_tiles)
        return c

    full_groups = band_tiles // u_tiles
    pairs = max(full_groups - 1, 0) // 2
    lax.fori_loop(0, pairs, stage_pair, 0)
    for g in range(2 * pairs, len(sizes)):
        has_next = g + 1 < len(sizes)
        stage(g % 2, g, sizes[g], g + 1 if has_next else None, sizes[g + 1] if has_next else 0)


def _suffix_matrix(n):
    row = lax.broadcasted_iota(jnp.int32, (n, n), 0)
    col = lax.broadcasted_iota(jnp.int32, (n, n), 1)
    half = jnp.concatenate([(row > col).astype(BF16), jnp.ones((n, n), BF16)], axis=1)
    return jnp.concatenate([half, half], axis=0)


def _sb_attention(qkv, batch, seq, weights_to_cast):
    n = qkv.shape[0]
    slabs = SB_HEADS // HEADS_PER_SLAB
    steps = batch * slabs
    tri = _suffix_matrix(SB_CHUNK)
    rows, nck = 2 * SB_TILE, SB_BAND // SB_CHUNK
    assert seq % SB_TILE == 0 and seq >= SB_BAND and SB_CHUNK == LANES
    cast_in_specs, cast_out_specs, cast_shapes = [], [], []
    for w, layer in weights_to_cast:
        w_rows, w_cols = w.shape[-2:]
        block = w_rows // steps
        assert w_rows % steps == 0 and block % BF16_SUBLANES == 0
        if w.ndim == 2:
            cast_in_specs.append(pl.BlockSpec((block, w_cols), lambda b, s: (b * slabs + s, 0)))
        else:
            cast_in_specs.append(pl.BlockSpec((None, block, w_cols), lambda b, s, layer=layer: (layer, b * slabs + s, 0)))
        cast_out_specs.append(pl.BlockSpec((block, w_cols), lambda b, s: (b * slabs + s, 0)))
        cast_shapes.append(jax.ShapeDtypeStruct((w_rows, w_cols), BF16))
    o, *cast = pl.pallas_call(
        functools.partial(_sb_kernel, n_cast=len(weights_to_cast)),
        out_shape=[jax.ShapeDtypeStruct((n, SB_HEADS * HEAD_DIM), BF16)] + cast_shapes,
        grid=(batch, slabs),
        in_specs=[
            pl.BlockSpec((seq, LANES), lambda b, s: (b, s)),
            pl.BlockSpec((seq, LANES), lambda b, s: (b, slabs + s)),
            pl.BlockSpec((seq, LANES), lambda b, s: (b, 2 * slabs + s)),
            _resident(tri.shape),
        ] + cast_in_specs,
        out_specs=[pl.BlockSpec((seq, LANES), lambda b, s: (b, s))] + cast_out_specs,
        scratch_shapes=[
            pltpu.VMEM((rows, SB_CHUNK), F32),
            pltpu.VMEM((2, SB_INTERLEAVE, rows, SB_BAND), F32),
            pltpu.VMEM((2, SB_INTERLEAVE, nck * rows, 2 * SB_CHUNK), BF16),
            pltpu.VMEM((2, SB_INTERLEAVE, rows, LANES), F32),
            pltpu.VMEM((2, SB_INTERLEAVE, rows, LANES), F32),
            pltpu.VMEM((rows, LANES), BF16),
            pltpu.VMEM((rows, LANES), F32),
            pltpu.VMEM((rows, LANES), F32),
        ],
        compiler_params=_params(("parallel", "parallel")),
        name="sb_attention",
    )(qkv, qkv, qkv, tri, *[w for w, _ in weights_to_cast])
    return o, cast


def _swa_kernel(sink_ref, bias_first_ref, bias_ref, q_ref, kvp_ref, kvc_ref, o_ref):
    w = WINDOW
    lane = lax.broadcasted_iota(jnp.int32, (w, LANES), 1)
    head0 = lane < HEAD_DIM
    zero = jnp.zeros((w, LANES), BF16)
    ones = jnp.ones((2 * w, LANES), BF16)
    for blk in range(SWA_BLOCKS_PER_STEP):
        rows = slice(blk * w, (blk + 1) * w)
        bias = jnp.concatenate([(bias_first_ref if blk == 0 else bias_ref)[0]] * SWA_GROUP, axis=0)
        for h in range(SWA_KV_HEADS):
            k_cols = slice(h * LANES, (h + 1) * LANES)
            v_cols = slice((SWA_KV_HEADS + h) * LANES, (SWA_KV_HEADS + h + 1) * LANES)
            if blk == 0:
                prev_k, prev_v = kvp_ref[:, k_cols], kvp_ref[:, v_cols]
            else:
                prev_k, prev_v = kvc_ref[(blk - 1) * w:blk * w, k_cols], kvc_ref[(blk - 1) * w:blk * w, v_cols]
            kk = jnp.concatenate([prev_k, kvc_ref[rows, k_cols]], axis=0)
            vv = jnp.concatenate([prev_v, kvc_ref[rows, v_cols]], axis=0)
            vv_ones = jnp.concatenate([vv, ones], axis=1)
            stacked, sinks = [], []
            for g in range(SWA_GROUP):
                head = h * SWA_GROUP + g
                slab = head // HEADS_PER_SLAB
                q = q_ref[rows, slab * LANES:(slab + 1) * LANES] * ATTN_SCALE
                stacked.append(jnp.where(head0, q, zero) if head % HEADS_PER_SLAB == 0 else jnp.where(head0, zero, q))
                sinks.append(jnp.full((w, LANES), sink_ref[head], F32))
            qs = jnp.concatenate(stacked, axis=0)
            sink = jnp.concatenate(sinks, axis=0)
            s = lax.dot_general(qs, kk, (((1,), (1,)), ((), ())), preferred_element_type=F32) + bias
            m = jnp.maximum(jnp.broadcast_to(jnp.max(s, axis=1, keepdims=True), sink.shape), sink)
            e = jnp.exp(s - jnp.concatenate([m, m], axis=1)).astype(BF16)
            ov = jnp.dot(e, vv_ones, preferred_element_type=F32)
            for pair in range(SWA_GROUP // HEADS_PER_SLAB):
                slab = (h * SWA_GROUP) // HEADS_PER_SLAB + pair
                lo, hi = slice(2 * pair * w, (2 * pair + 1) * w), slice((2 * pair + 1) * w, (2 * pair + 2) * w)
                both = lambda x: jnp.where(head0, x[lo], x[hi])
                denom = both(ov[:, LANES:]) + jnp.exp(both(sink) - both(m))
                o_ref[rows, slab * LANES:(slab + 1) * LANES] = (both(ov[:, :LANES]) * (1.0 / denom)).astype(o_ref.dtype)


def _swa_attention(q, kv, sinks, batch, seq):
    n, dq = q.shape
    step_rows = SWA_BLOCKS_PER_STEP * WINDOW
    steps = seq // step_rows
    dkv = kv.shape[1]
    qi = lax.broadcasted_iota(jnp.int32, (WINDOW, 2 * WINDOW), 0)
    ki = lax.broadcasted_iota(jnp.int32, (WINDOW, 2 * WINDOW), 1)
    diff = qi + WINDOW - ki
    in_window = (diff >= 0) & (diff < WINDOW)
    bias = jnp.where(jnp.stack([in_window & (ki >= WINDOW), in_window]), 0.0, -jnp.inf).astype(F32)
    bias_spec = lambda index: pl.BlockSpec((1, WINDOW, 2 * WINDOW), index)
    return pl.pallas_call(
        _swa_kernel,
        out_shape=jax.ShapeDtypeStruct((n, dq), BF16),
        grid=(batch, steps),
        in_specs=[
            pl.BlockSpec(memory_space=pltpu.SMEM),
            bias_spec(lambda b, i: (jnp.minimum(i, 1), 0, 0)),
            bias_spec(lambda b, i: (1, 0, 0)),
            pl.BlockSpec((step_rows, dq), lambda b, i: (b * steps + i, 0)),
            pl.BlockSpec((WINDOW, dkv),
                         lambda b, i: ((b * steps + i) * SWA_BLOCKS_PER_STEP - jnp.minimum(i, 1), 0)),
            pl.BlockSpec((step_rows, dkv), lambda b, i: (b * steps + i, 0)),
        ],
        out_specs=pl.BlockSpec((step_rows, dq), lambda b, i: (b * steps + i, 0)),
        compiler_params=_params(("parallel", "arbitrary")),
        name="swa_attention",
    )(sinks, bias, bias, q, kv, kv)


def kernel(x, ffn1_norm, ffn1_w_in, ffn1_w_out, mix_norm, ffn2_norm, ffn2_w_in, ffn2_w_out,
           sb_w_qkv, sb_w_o, kv_norm, kv_w, swa_w_q, swa_sinks, swa_w_o, final_norm):
    batch, seq, d = x.shape
    n = batch * seq
    cos, sin = _rotary_tables(seq)
    bf = _to_bf16
    h = x.reshape(n, d)

    h, w_qkv = _ffn(h, ffn1_norm[0], bf(ffn1_w_in, 0), bf(ffn1_w_out, 0), side_cast=(sb_w_qkv, 0), name="ffn1_l0")
    qkv = _proj(h, mix_norm[0], w_qkv, cos, sin, rot_slabs=0, name="proj_qkv")
    later_weights = [(ffn2_w_in, 0), (ffn2_w_out, 0), (sb_w_o, 0), (kv_w, None), (ffn1_w_in, 1), (ffn1_w_out, 1),
                     (swa_w_q, 0), (ffn2_w_in, 1), (ffn2_w_out, 1), (swa_w_o, 0)]
    o, (w2_in0, w2_out0, w_o0, w_kv, w1_in1, w1_out1, w_q1, w2_in1, w2_out1, w_o1) = _sb_attention(
        qkv, batch, seq, later_weights)
    h = _ffn(h, ffn2_norm[0], w2_in0, w2_out0, attn=(o, w_o0), name="ffn2_l0")
    k_slabs = SWA_KV_HEADS * HEAD_DIM // LANES
    kv = _proj(h, kv_norm, w_kv, cos, sin, rot_slabs=k_slabs, name="proj_kv", dup_heads=True)

    h = _ffn(h, ffn1_norm[1], w1_in1, w1_out1, name="ffn1_l1")
    q = _proj(h, mix_norm[1], w_q1, cos, sin, rot_slabs=SWA_Q_HEADS * HEAD_DIM // LANES, name="proj_q")
    o = _swa_attention(q, kv, swa_sinks[0], batch, seq)
    h = _ffn(h, ffn2_norm[1], w2_in1, w2_out1, attn=(o, w_o1), final_g=final_norm, name="ffn2_l1")
    return h.reshape(batch, seq, d)
```

```python
import functools

import jax
import jax.numpy as jnp
from jax import lax
from jax.experimental import pallas as pl
from jax.experimental.pallas import tpu as pltpu

F32 = jnp.float32
BF16 = jnp.bfloat16

HEAD_DIM = 64
SB_HEADS = 16
SWA_Q_HEADS = 16
SWA_KV_HEADS = 4
SWA_GROUP = SWA_Q_HEADS // SWA_KV_HEADS
WINDOW = 128
D_FF = 2816
ROPE_THETA = 10000.0
RMS_EPS = 1e-6
FFN_RES_SCALE = 0.5
ATTN_SCALE = HEAD_DIM ** -0.5
LOG2E = 1.4426950408889634

LANES = 128
BF16_SUBLANES = 16
HEADS_PER_SLAB = LANES // HEAD_DIM
VMEM_LIMIT_BYTES = 56 * 1024 * 1024

TOKEN_TILE = 1024
PROJ_CHUNK = 256
CAST_BLOCK_BYTES = 4 * 1024 * 1024
FFN_TILE = 1024
FF_CHUNKS = 11
SWA_BLOCKS_PER_STEP = 8
SB_TILE = 64
SB_BAND = 256
SB_CHUNK = 128
SB_INTERLEAVE = 16
F32_EXP_UNDERFLOW = -105.0


def _rms(x, g):
    return x * lax.rsqrt(jnp.mean(x * x, axis=-1, keepdims=True) + RMS_EPS) * g


def _params(semantics):
    return pltpu.CompilerParams(dimension_semantics=semantics, vmem_limit_bytes=VMEM_LIMIT_BYTES)


def _resident(shape):
    return pl.BlockSpec(shape, lambda *_: (0,) * len(shape), pipeline_mode=pl.Buffered(1))


def _cast_kernel(w_ref, out_ref):
    out_ref[...] = w_ref[...].astype(out_ref.dtype)


def _to_bf16(w, layer=None, name="cast"):
    rows, cols = w.shape[-2:]
    block_rows = max(BF16_SUBLANES, min(rows, CAST_BLOCK_BYTES // (4 * cols) // BF16_SUBLANES * BF16_SUBLANES))
    while rows % block_rows:
        block_rows -= BF16_SUBLANES
    if layer is None:
        in_spec = pl.BlockSpec((block_rows, cols), lambda i: (i, 0))
    else:
        in_spec = pl.BlockSpec((None, block_rows, cols), lambda i: (layer, i, 0))
    return pl.pallas_call(
        _cast_kernel,
        out_shape=jax.ShapeDtypeStruct((rows, cols), BF16),
        grid=(rows // block_rows,),
        in_specs=[in_spec],
        out_specs=pl.BlockSpec((block_rows, cols), lambda i: (i, 0)),
        compiler_params=_params(("parallel",)),
        name=name,
    )(w)


def _ffn_kernel(*refs, has_attn, has_final_norm, has_side_cast):
    refs = list(refs)
    h_ref = refs.pop(0)
    if has_attn:
        o_ref, wo_ref = refs.pop(0), refs.pop(0)
    g_ref, win_ref, wout_ref = refs.pop(0), refs.pop(0), refs.pop(0)
    if has_final_norm:
        fg_ref = refs.pop(0)
    if has_side_cast:
        side_ref, out_ref, side_out_ref = refs
        side_out_ref[...] = side_ref[...].astype(side_out_ref.dtype)
    else:
        (out_ref,) = refs

    x = h_ref[...]
    if has_attn:
        x = x + jnp.dot(o_ref[...], wo_ref[...], preferred_element_type=F32)
    xn = _rms(x, g_ref[...]).astype(BF16)
    fc = D_FF // FF_CHUNKS
    y = None
    for c in range(FF_CHUNKS):
        gate = jnp.dot(xn, win_ref[:, c * fc:(c + 1) * fc], preferred_element_type=F32)
        up = jnp.dot(xn, win_ref[:, D_FF + c * fc:D_FF + (c + 1) * fc], preferred_element_type=F32)
        act = (gate * jax.nn.sigmoid(gate) * up).astype(BF16)
        yc = jnp.dot(act, wout_ref[c * fc:(c + 1) * fc, :], preferred_element_type=F32)
        y = yc if y is None else y + yc
    x = x + FFN_RES_SCALE * y
    if has_final_norm:
        x = _rms(x, fg_ref[...])
    out_ref[...] = x


def _ffn(h, g, w_in, w_out, attn=None, final_g=None, side_cast=None, name="ffn"):
    n, d = h.shape
    steps = n // FFN_TILE
    row = pl.BlockSpec((FFN_TILE, d), lambda i: (i, 0))
    args, specs = [h], [row]
    if attn is not None:
        o, w_o = attn
        args += [o, w_o]
        specs += [row, _resident(w_o.shape)]
    args += [g.reshape(1, d), w_in, w_out]
    specs += [_resident((1, d)), _resident(w_in.shape), _resident(w_out.shape)]
    if final_g is not None:
        args.append(final_g.reshape(1, d))
        specs.append(_resident((1, d)))
    out_shape, out_specs = jax.ShapeDtypeStruct((n, d), F32), row
    if side_cast is not None:
        w, layer = side_cast
        w_rows, w_cols = w.shape[-2:]
        block = w_rows // steps
        assert w_rows % steps == 0 and block % BF16_SUBLANES == 0
        args.append(w)
        specs.append(pl.BlockSpec((None, block, w_cols), lambda i: (layer, i, 0)))
        out_shape = [out_shape, jax.ShapeDtypeStruct((w_rows, w_cols), BF16)]
        out_specs = [row, pl.BlockSpec((block, w_cols), lambda i: (i, 0))]
    return pl.pallas_call(
        functools.partial(_ffn_kernel, has_attn=attn is not None, has_final_norm=final_g is not None,
                          has_side_cast=side_cast is not None),
        out_shape=out_shape,
        grid=(steps,),
        in_specs=specs,
        out_specs=out_specs,
        compiler_params=_params(("parallel",)),
        name=name,
    )(*args)


def _proj_kernel(h_ref, g_ref, w_ref, cos_ref, sin_ref, out_ref, *, rot_slabs, dup_heads):
    xn = _rms(h_ref[...], g_ref[...]).astype(BF16)
    cos, sin = cos_ref[...], sin_ref[...]
    lane = lax.broadcasted_iota(jnp.int32, cos.shape, 1)
    first_half = (lane % HEAD_DIM) < (HEAD_DIM // 2)
    head0 = lane < HEAD_DIM
    slabs_per_chunk = PROJ_CHUNK // LANES
    for s in range(w_ref.shape[1] // LANES):
        if s % slabs_per_chunk == 0:
            y = jnp.dot(xn, w_ref[:, s * LANES:(s + slabs_per_chunk) * LANES], preferred_element_type=F32)
        ys = y[:, (s % slabs_per_chunk) * LANES:(s % slabs_per_chunk + 1) * LANES]
        if s < rot_slabs:
            partner = jnp.where(first_half,
                                pltpu.roll(ys, LANES - HEAD_DIM // 2, 1),
                                pltpu.roll(ys, HEAD_DIM // 2, 1))
            ys = ys * cos + partner * sin
        if dup_heads:
            swapped = pltpu.roll(ys, HEAD_DIM, 1)
            out_ref[:, 2 * s * LANES:(2 * s + 1) * LANES] = jnp.where(head0, ys, swapped).astype(out_ref.dtype)
            out_ref[:, (2 * s + 1) * LANES:(2 * s + 2) * LANES] = jnp.where(head0, swapped, ys).astype(out_ref.dtype)
        else:
            out_ref[:, s * LANES:(s + 1) * LANES] = ys.astype(out_ref.dtype)


def _proj(h, g, w, cos, sin, rot_slabs, name, dup_heads=False):
    n, d = h.shape
    m = w.shape[1] * (HEADS_PER_SLAB if dup_heads else 1)
    tiles_per_seq = cos.shape[0] // TOKEN_TILE
    table = pl.BlockSpec((TOKEN_TILE, LANES), lambda i: (i % tiles_per_seq, 0))
    return pl.pallas_call(
        functools.partial(_proj_kernel, rot_slabs=rot_slabs, dup_heads=dup_heads),
        out_shape=jax.ShapeDtypeStruct((n, m), BF16),
        grid=(n // TOKEN_TILE,),
        in_specs=[pl.BlockSpec((TOKEN_TILE, d), lambda i: (i, 0)), _resident((1, d)), _resident(w.shape),
                  table, table],
        out_specs=pl.BlockSpec((TOKEN_TILE, m), lambda i: (i, 0)),
        compiler_params=_params(("parallel",)),
        name=name,
    )(h, g.reshape(1, d), w, cos, sin)


def _rotary_tables(seq):
    half = HEAD_DIM // 2
    inv_freq = ROPE_THETA ** (-jnp.arange(half, dtype=F32) / half)
    ang = jnp.arange(seq, dtype=F32)[:, None] * inv_freq[None, :]
    cos, sin = jnp.cos(ang), jnp.sin(ang)
    cos_head = jnp.concatenate([cos, cos], axis=-1)
    sin_head = jnp.concatenate([-sin, sin], axis=-1)
    return jnp.tile(cos_head, (1, HEADS_PER_SLAB)), jnp.tile(sin_head, (1, HEADS_PER_SLAB))


def _sb_kernel(*refs, n_cast):
    q_ref, k_ref, v_ref, tri_ref = refs[:4]
    cast_in = refs[4:4 + n_cast]
    o_ref = refs[4 + n_cast]
    cast_out = refs[5 + n_cast:5 + 2 * n_cast]
    mask_ref, lb_ref, split_ref, acc_ref, carry_ref, gqs_ref, gacc_ref, gcarry_ref = refs[5 + 2 * n_cast:]
    for w_ref, w_out_ref in zip(cast_in, cast_out):
        w_out_ref[...] = w_ref[...].astype(w_out_ref.dtype)

    t, band, ck, u_tiles = SB_TILE, SB_BAND, SB_CHUNK, SB_INTERLEAVE
    nck = band // ck
    seq = q_ref.shape[0]
    q_tiles = seq // t
    head_tiles = band // t
    band_tiles = q_tiles - head_tiles
    sizes = [u_tiles] * (band_tiles // u_tiles) + ([band_tiles % u_tiles] if band_tiles % u_tiles else [])
    lane = lax.broadcasted_iota(jnp.int32, (t, LANES), 1)
    head0 = lane < HEAD_DIM
    qrow = lax.broadcasted_iota(jnp.int32, (2 * t, ck), 0) & (t - 1)
    col = lax.broadcasted_iota(jnp.int32, (2 * t, ck), 1)
    mask_ref[...] = jnp.where(col < qrow + (band - t - (nck - 1) * ck), 1.0, 0.0)

    def aligned(offset):
        return offset if isinstance(offset, int) else pl.multiple_of(offset, t)

    def stacked_queries(i):
        q = q_ref[pl.ds(aligned(i * t), t), :] * ATTN_SCALE
        zero = jnp.zeros_like(q)
        return jnp.concatenate([jnp.where(head0, q, zero), jnp.where(head0, zero, q)], axis=0)

    def key_rows(start):
        return pl.ds(aligned(start), band)

    def logits(qs, start):
        z = lax.dot_general(qs, k_ref[key_rows(start), :], (((1,), (1,)), ((), ())),
                            preferred_element_type=F32)
        sp = jnp.log(1.0 + jnp.exp2(jnp.abs(z) * -LOG2E))
        log_beta = jnp.minimum(z, 0.0) - sp
        return log_beta, log_beta - z

    def split_chunk(l):
        hi = l.astype(BF16)
        lo = (l - hi.astype(F32)).astype(BF16)
        return jnp.concatenate([hi, lo], axis=1)

    def suffix_sums(split):
        return jnp.dot(split, tri_ref[...], preferred_element_type=F32)

    def chunk(x, c):
        return x[:, c * ck:(c + 1) * ck]

    def write_output(i, acc):
        o_ref[pl.ds(aligned(i * t), t), :] = jnp.where(head0, acc[:t], acc[t:]).astype(o_ref.dtype)

    def band_start(i):
        return (i - (head_tiles - 1)) * t

    def sweep_start(i, head):
        return 0 if head else band_start(i)

    def causal(x, c, i, head):
        if head:
            return jnp.where(col + c * ck < qrow + i * t, x, 0.0)
        return x * mask_ref[...] if c == nck - 1 else x

    def logits_stage(parity, slot, i, head=False):
        log_beta, log_1m = logits(stacked_queries(i), sweep_start(i, head))
        lb_ref[parity, slot] = log_beta
        for c in range(nck):
            split_ref[parity, slot, c * 2 * t:(c + 1) * 2 * t, :] = split_chunk(causal(chunk(log_1m, c), c, i, head))

    def weights_stage(parity, slot, i, sums, head=False):
        log_beta = lb_ref[parity, slot]
        later = None
        ws = [None] * nck
        for c in reversed(range(nck)):
            sums_c = sums[c * 2 * t:(c + 1) * 2 * t]
            x = chunk(log_beta, c) + chunk(sums_c, 0)
            if later is not None:
                x = x + later
            ws[c] = causal(jnp.exp(x), c, i, head).astype(BF16)
            later = chunk(sums_c, 1) if later is None else later + chunk(sums_c, 1)
        acc = jnp.dot(jnp.concatenate(ws, axis=1), v_ref[key_rows(sweep_start(i, head)), :],
                      preferred_element_type=F32)
        acc_ref[parity, slot] = acc
        carry_ref[parity, slot] = later
        write_output(i, acc)
        return later

    def generic_sweep(i, start, limit):
        log_beta, log_1m = logits(gqs_ref[...], start)
        bound = jnp.minimum(i * t + qrow, limit) - start
        valids = [col + c * ck < bound for c in range(nck)]
        sums = suffix_sums(jnp.concatenate(
            [split_chunk(jnp.where(valids[c], chunk(log_1m, c), 0.0)) for c in range(nck)], axis=0))
        later = gcarry_ref[...]
        ws = [None] * nck
        for c in reversed(range(nck)):
            sums_c = sums[c * 2 * t:(c + 1) * 2 * t]
            x = chunk(log_beta, c) + chunk(sums_c, 0) + later
            ws[c] = jnp.where(valids[c], jnp.exp(x), 0.0).astype(BF16)
            later = later + chunk(sums_c, 1)
        gacc_ref[...] += jnp.dot(jnp.concatenate(ws, axis=1), v_ref[key_rows(start), :],
                                 preferred_element_type=F32)
        gcarry_ref[...] = later
        return jnp.max(later)

    def continue_tile(parity, slot, i):
        gqs_ref[...] = stacked_queries(i)
        gacc_ref[...] = acc_ref[parity, slot]
        gcarry_ref[...] = carry_ref[parity, slot]

        def cond(state):
            limit, top = state
            return jnp.logical_and(limit > 0, top > F32_EXP_UNDERFLOW)

        def body(state):
            limit, _ = state
            start = jnp.maximum(limit - band, 0)
            return start, generic_sweep(i, start, limit)

        lax.while_loop(cond, body, (band_start(i), jnp.max(gcarry_ref[...])))
        write_output(i, gacc_ref[...])

    def first_tile(g):
        return head_tiles + (sum(sizes[:g]) if isinstance(g, int) else g * u_tiles)

    def stage(parity, group, size, next_group, next_size):
        sums = [suffix_sums(split_ref[parity, slot]) for slot in range(size)]
        if next_group is not None:
            for slot in range(next_size):
                logits_stage(1 - parity, slot, first_tile(next_group) + slot)
        top = None
        for slot in range(size):
            carry = weights_stage(parity, slot, first_tile(group) + slot, sums[slot])
            top = carry if top is None else jnp.maximum(top, carry)

        @pl.when(jnp.max(top) > F32_EXP_UNDERFLOW)
        def _():
            def one(slot, c):
                continue_tile(parity, slot, first_tile(group) + slot)
                return c
            lax.fori_loop(0, size, one, 0)

    for i in range(head_tiles):
        logits_stage(1, i, i, head=True)
    for i in range(head_tiles):
        weights_stage(1, i, i, suffix_sums(split_ref[1, i]), head=True)
    for slot in range(sizes[0]):
        logits_stage(0, slot, first_tile(0) + slot)

    def stage_pair(m, c):
        stage(0, 2 * m, u_tiles, 2 * m + 1, u_tiles)
        stage(1, 2 * m + 1, u_tiles, 2 * m + 2, u_tiles)
        return c

    full_groups = band_tiles // u_tiles
    pairs = max(full_groups - 1, 0) // 2
    lax.fori_loop(0, pairs, stage_pair, 0)
    for g in range(2 * pairs, len(sizes)):
        has_next = g + 1 < len(sizes)
        stage(g % 2, g, sizes[g], g + 1 if has_next else None, sizes[g + 1] if has_next else 0)


def _suffix_matrix(n):
    row = lax.broadcasted_iota(jnp.int32, (n, n), 0)
    col = lax.broadcasted_iota(jnp.int32, (n, n), 1)
    half = jnp.concatenate([(row > col).astype(BF16), jnp.ones((n, n), BF16)], axis=1)
    return jnp.concatenate([half, half], axis=0)


def _sb_attention(qkv, batch, seq, weights_to_cast):
    n = qkv.shape[0]
    slabs = SB_HEADS // HEADS_PER_SLAB
    steps = batch * slabs
    tri = _suffix_matrix(SB_CHUNK)
    rows, nck = 2 * SB_TILE, SB_BAND // SB_CHUNK
    assert seq % SB_TILE == 0 and seq >= SB_BAND and SB_CHUNK == LANES and SB_BAND // SB_TILE <= SB_INTERLEAVE
    cast_in_specs, cast_out_specs, cast_shapes = [], [], []
    for w, layer in weights_to_cast:
        w_rows, w_cols = w.shape[-2:]
        block = w_rows // steps
        assert w_rows % steps == 0 and block % BF16_SUBLANES == 0
        if w.ndim == 2:
            cast_in_specs.append(pl.BlockSpec((block, w_cols), lambda b, s: (b * slabs + s, 0)))
        else:
            cast_in_specs.append(pl.BlockSpec((None, block, w_cols), lambda b, s, layer=layer: (layer, b * slabs + s, 0)))
        cast_out_specs.append(pl.BlockSpec((block, w_cols), lambda b, s: (b * slabs + s, 0)))
        cast_shapes.append(jax.ShapeDtypeStruct((w_rows, w_cols), BF16))
    o, *cast = pl.pallas_call(
        functools.partial(_sb_kernel, n_cast=len(weights_to_cast)),
        out_shape=[jax.ShapeDtypeStruct((n, SB_HEADS * HEAD_DIM), BF16)] + cast_shapes,
        grid=(batch, slabs),
        in_specs=[
            pl.BlockSpec((seq, LANES), lambda b, s: (b, s)),
            pl.BlockSpec((seq, LANES), lambda b, s: (b, slabs + s)),
            pl.BlockSpec((seq, LANES), lambda b, s: (b, 2 * slabs + s)),
            _resident(tri.shape),
        ] + cast_in_specs,
        out_specs=[pl.BlockSpec((seq, LANES), lambda b, s: (b, s))] + cast_out_specs,
        scratch_shapes=[
            pltpu.VMEM((rows, SB_CHUNK), F32),
            pltpu.VMEM((2, SB_INTERLEAVE, rows, SB_BAND), F32),
            pltpu.VMEM((2, SB_INTERLEAVE, nck * rows, 2 * SB_CHUNK), BF16),
            pltpu.VMEM((2, SB_INTERLEAVE, rows, LANES), F32),
            pltpu.VMEM((2, SB_INTERLEAVE, rows, LANES), F32),
            pltpu.VMEM((rows, LANES), BF16),
            pltpu.VMEM((rows, LANES), F32),
            pltpu.VMEM((rows, LANES), F32),
        ],
        compiler_params=_params(("parallel", "parallel")),
        name="sb_attention",
    )(qkv, qkv, qkv, tri, *[w for w, _ in weights_to_cast])
    return o, cast


def _swa_kernel(sink_ref, bias_first_ref, bias_ref, q_ref, kvp_ref, kvc_ref, o_ref):
    w = WINDOW
    lane = lax.broadcasted_iota(jnp.int32, (w, LANES), 1)
    head0 = lane < HEAD_DIM
    zero = jnp.zeros((w, LANES), BF16)
    ones = jnp.ones((2 * w, LANES), BF16)
    for blk in range(SWA_BLOCKS_PER_STEP):
        rows = slice(blk * w, (blk + 1) * w)
        bias = jnp.concatenate([(bias_first_ref if blk == 0 else bias_ref)[0]] * SWA_GROUP, axis=0)
        for h in range(SWA_KV_HEADS):
            k_cols = slice(h * LANES, (h + 1) * LANES)
            v_cols = slice((SWA_KV_HEADS + h) * LANES, (SWA_KV_HEADS + h + 1) * LANES)
            if blk == 0:
                prev_k, prev_v = kvp_ref[:, k_cols], kvp_ref[:, v_cols]
            else:
                prev_k, prev_v = kvc_ref[(blk - 1) * w:blk * w, k_cols], kvc_ref[(blk - 1) * w:blk * w, v_cols]
            kk = jnp.concatenate([prev_k, kvc_ref[rows, k_cols]], axis=0)
            vv = jnp.concatenate([prev_v, kvc_ref[rows, v_cols]], axis=0)
            vv_ones = jnp.concatenate([vv, ones], axis=1)
            stacked, sinks = [], []
            for g in range(SWA_GROUP):
                head = h * SWA_GROUP + g
                slab = head // HEADS_PER_SLAB
                q = q_ref[rows, slab * LANES:(slab + 1) * LANES] * ATTN_SCALE
                stacked.append(jnp.where(head0, q, zero) if head % HEADS_PER_SLAB == 0 else jnp.where(head0, zero, q))
                sinks.append(jnp.full((w, LANES), sink_ref[head], F32))
            qs = jnp.concatenate(stacked, axis=0)
            sink = jnp.concatenate(sinks, axis=0)
            s = lax.dot_general(qs, kk, (((1,), (1,)), ((), ())), preferred_element_type=F32) + bias
            m = jnp.maximum(jnp.broadcast_to(jnp.max(s, axis=1, keepdims=True), sink.shape), sink)
            e = jnp.exp(s - jnp.concatenate([m, m], axis=1)).astype(BF16)
            ov = jnp.dot(e, vv_ones, preferred_element_type=F32)
            for pair in range(SWA_GROUP // HEADS_PER_SLAB):
                slab = (h * SWA_GROUP) // HEADS_PER_SLAB + pair
                lo, hi = slice(2 * pair * w, (2 * pair + 1) * w), slice((2 * pair + 1) * w, (2 * pair + 2) * w)
                both = lambda x: jnp.where(head0, x[lo], x[hi])
                denom = both(ov[:, LANES:]) + jnp.exp(both(sink) - both(m))
                o_ref[rows, slab * LANES:(slab + 1) * LANES] = (both(ov[:, :LANES]) * (1.0 / denom)).astype(o_ref.dtype)


def _swa_attention(q, kv, sinks, batch, seq):
    n, dq = q.shape
    step_rows = SWA_BLOCKS_PER_STEP * WINDOW
    steps = seq // step_rows
    dkv = kv.shape[1]
    qi = lax.broadcasted_iota(jnp.int32, (WINDOW, 2 * WINDOW), 0)
    ki = lax.broadcasted_iota(jnp.int32, (WINDOW, 2 * WINDOW), 1)
    diff = qi + WINDOW - ki
    in_window = (diff >= 0) & (diff < WINDOW)
    bias = jnp.where(jnp.stack([in_window & (ki >= WINDOW), in_window]), 0.0, -jnp.inf).astype(F32)
    bias_spec = lambda index: pl.BlockSpec((1, WINDOW, 2 * WINDOW), index)
    return pl.pallas_call(
        _swa_kernel,
        out_shape=jax.ShapeDtypeStruct((n, dq), BF16),
        grid=(batch, steps),
        in_specs=[
            pl.BlockSpec(memory_space=pltpu.SMEM),
            bias_spec(lambda b, i: (jnp.minimum(i, 1), 0, 0)),
            bias_spec(lambda b, i: (1, 0, 0)),
            pl.BlockSpec((step_rows, dq), lambda b, i: (b * steps + i, 0)),
            pl.BlockSpec((WINDOW, dkv),
                         lambda b, i: ((b * steps + i) * SWA_BLOCKS_PER_STEP - jnp.minimum(i, 1), 0)),
            pl.BlockSpec((step_rows, dkv), lambda b, i: (b * steps + i, 0)),
        ],
        out_specs=pl.BlockSpec((step_rows, dq), lambda b, i: (b * steps + i, 0)),
        compiler_params=_params(("parallel", "arbitrary")),
        name="swa_attention",
    )(sinks, bias, bias, q, kv, kv)


def kernel(x, ffn1_norm, ffn1_w_in, ffn1_w_out, mix_norm, ffn2_norm, ffn2_w_in, ffn2_w_out,
           sb_w_qkv, sb_w_o, kv_norm, kv_w, swa_w_q, swa_sinks, swa_w_o, final_norm):
    batch, seq, d = x.shape
    n = batch * seq
    cos, sin = _rotary_tables(seq)
    bf = _to_bf16
    h = x.reshape(n, d)

    h, w_qkv = _ffn(h, ffn1_norm[0], bf(ffn1_w_in, 0), bf(ffn1_w_out, 0), side_cast=(sb_w_qkv, 0), name="ffn1_l0")
    qkv = _proj(h, mix_norm[0], w_qkv, cos, sin, rot_slabs=0, name="proj_qkv")
    later_weights = [(ffn2_w_in, 0), (ffn2_w_out, 0), (sb_w_o, 0), (kv_w, None), (ffn1_w_in, 1), (ffn1_w_out, 1),
                     (swa_w_q, 0), (ffn2_w_in, 1), (ffn2_w_out, 1), (swa_w_o, 0)]
    o, (w2_in0, w2_out0, w_o0, w_kv, w1_in1, w1_out1, w_q1, w2_in1, w2_out1, w_o1) = _sb_attention(
        qkv, batch, seq, later_weights)
    h = _ffn(h, ffn2_norm[0], w2_in0, w2_out0, attn=(o, w_o0), name="ffn2_l0")
    k_slabs = SWA_KV_HEADS * HEAD_DIM // LANES
    kv = _proj(h, kv_norm, w_kv, cos, sin, rot_slabs=k_slabs, name="proj_kv", dup_heads=True)

    h = _ffn(h, ffn1_norm[1], w1_in1, w1_out1, name="ffn1_l1")
    q = _proj(h, mix_norm[1], w_q1, cos, sin, rot_slabs=SWA_Q_HEADS * HEAD_DIM // LANES, name="proj_q")
    o = _swa_attention(q, kv, swa_sinks[0], batch, seq)
    h = _ffn(h, ffn2_norm[1], w2_in1, w2_out1, attn=(o, w_o1), final_g=final_norm, name="ffn2_l1")
    return h.reshape(batch, seq, d)
```

```python
import functools

import jax
import jax.numpy as jnp
from jax import lax
from jax.experimental import pallas as pl
from jax.experimental.pallas import tpu as pltpu

F32 = jnp.float32
BF16 = jnp.bfloat16

HEAD_DIM = 64
SB_HEADS = 16
SWA_Q_HEADS = 16
SWA_KV_HEADS = 4
SWA_GROUP = SWA_Q_HEADS // SWA_KV_HEADS
WINDOW = 128
D_FF = 2816
ROPE_THETA = 10000.0
RMS_EPS = 1e-6
FFN_RES_SCALE = 0.5
ATTN_SCALE = HEAD_DIM ** -0.5
LOG2E = 1.4426950408889634

LANES = 128
BF16_SUBLANES = 16
HEADS_PER_SLAB = LANES // HEAD_DIM
VMEM_LIMIT_BYTES = 56 * 1024 * 1024

PROJ_TILE_ELEMS = 3 * 1024 * 1024
PROJ_CHUNK = 256
CAST_BLOCK_BYTES = 4 * 1024 * 1024
FFN_TILE = 1024
FF_CHUNKS = 11
SWA_BLOCKS_PER_STEP = 8
SB_TILE = 64
SB_BAND = 256
SB_CHUNK = 128
SB_INTERLEAVE = 16
F32_EXP_UNDERFLOW = -105.0


def _rms(x, g):
    return x * lax.rsqrt(jnp.mean(x * x, axis=-1, keepdims=True) + RMS_EPS) * g


def _params(semantics):
    return pltpu.CompilerParams(dimension_semantics=semantics, vmem_limit_bytes=VMEM_LIMIT_BYTES)


def _resident(shape):
    return pl.BlockSpec(shape, lambda *_: (0,) * len(shape), pipeline_mode=pl.Buffered(1))


def _cast_kernel(w_ref, out_ref):
    out_ref[...] = w_ref[...].astype(out_ref.dtype)


def _to_bf16(w, layer=None, name="cast"):
    rows, cols = w.shape[-2:]
    block_rows = max(BF16_SUBLANES, min(rows, CAST_BLOCK_BYTES // (4 * cols) // BF16_SUBLANES * BF16_SUBLANES))
    while rows % block_rows:
        block_rows -= BF16_SUBLANES
    if layer is None:
        in_spec = pl.BlockSpec((block_rows, cols), lambda i: (i, 0))
    else:
        in_spec = pl.BlockSpec((None, block_rows, cols), lambda i: (layer, i, 0))
    return pl.pallas_call(
        _cast_kernel,
        out_shape=jax.ShapeDtypeStruct((rows, cols), BF16),
        grid=(rows // block_rows,),
        in_specs=[in_spec],
        out_specs=pl.BlockSpec((block_rows, cols), lambda i: (i, 0)),
        compiler_params=_params(("parallel",)),
        name=name,
    )(w)


def _ffn_kernel(*refs, has_attn, has_final_norm, has_side_cast):
    refs = list(refs)
    h_ref = refs.pop(0)
    if has_attn:
        o_ref, wo_ref = refs.pop(0), refs.pop(0)
    g_ref, win_ref, wout_ref = refs.pop(0), refs.pop(0), refs.pop(0)
    if has_final_norm:
        fg_ref = refs.pop(0)
    if has_side_cast:
        side_ref, out_ref, side_out_ref = refs
        side_out_ref[...] = side_ref[...].astype(side_out_ref.dtype)
    else:
        (out_ref,) = refs

    x = h_ref[...]
    if has_attn:
        x = x + jnp.dot(o_ref[...], wo_ref[...], preferred_element_type=F32)
    xn = _rms(x, g_ref[...]).astype(BF16)
    fc = D_FF // FF_CHUNKS
    y = None
    for c in range(FF_CHUNKS):
        gate = jnp.dot(xn, win_ref[:, c * fc:(c + 1) * fc], preferred_element_type=F32)
        up = jnp.dot(xn, win_ref[:, D_FF + c * fc:D_FF + (c + 1) * fc], preferred_element_type=F32)
        act = (gate * jax.nn.sigmoid(gate) * up).astype(BF16)
        yc = jnp.dot(act, wout_ref[c * fc:(c + 1) * fc, :], preferred_element_type=F32)
        y = yc if y is None else y + yc
    x = x + FFN_RES_SCALE * y
    if has_final_norm:
        x = _rms(x, fg_ref[...])
    out_ref[...] = x


def _ffn(h, g, w_in, w_out, attn=None, final_g=None, side_cast=None, name="ffn"):
    n, d = h.shape
    steps = n // FFN_TILE
    row = pl.BlockSpec((FFN_TILE, d), lambda i: (i, 0))
    args, specs = [h], [row]
    if attn is not None:
        o, w_o = attn
        args += [o, w_o]
        specs += [row, _resident(w_o.shape)]
    args += [g.reshape(1, d), w_in, w_out]
    specs += [_resident((1, d)), _resident(w_in.shape), _resident(w_out.shape)]
    if final_g is not None:
        args.append(final_g.reshape(1, d))
        specs.append(_resident((1, d)))
    out_shape, out_specs = jax.ShapeDtypeStruct((n, d), F32), row
    if side_cast is not None:
        w, layer = side_cast
        w_rows, w_cols = w.shape[-2:]
        block = w_rows // steps
        assert w_rows % steps == 0 and block % BF16_SUBLANES == 0
        args.append(w)
        specs.append(pl.BlockSpec((None, block, w_cols), lambda i: (layer, i, 0)))
        out_shape = [out_shape, jax.ShapeDtypeStruct((w_rows, w_cols), BF16)]
        out_specs = [row, pl.BlockSpec((block, w_cols), lambda i: (i, 0))]
    return pl.pallas_call(
        functools.partial(_ffn_kernel, has_attn=attn is not None, has_final_norm=final_g is not None,
                          has_side_cast=side_cast is not None),
        out_shape=out_shape,
        grid=(steps,),
        in_specs=specs,
        out_specs=out_specs,
        compiler_params=_params(("parallel",)),
        name=name,
    )(*args)


def _proj_kernel(h_ref, g_ref, w_ref, cos_ref, sin_ref, out_ref, *, rot_slabs, dup_heads):
    xn = _rms(h_ref[...], g_ref[...]).astype(BF16)
    cos, sin = cos_ref[...], sin_ref[...]
    lane = lax.broadcasted_iota(jnp.int32, cos.shape, 1)
    first_half = (lane % HEAD_DIM) < (HEAD_DIM // 2)
    head0 = lane < HEAD_DIM
    slabs_per_chunk = PROJ_CHUNK // LANES
    for s in range(w_ref.shape[1] // LANES):
        if s % slabs_per_chunk == 0:
            y = jnp.dot(xn, w_ref[:, s * LANES:(s + slabs_per_chunk) * LANES], preferred_element_type=F32)
        ys = y[:, (s % slabs_per_chunk) * LANES:(s % slabs_per_chunk + 1) * LANES]
        if s < rot_slabs:
            partner = jnp.where(first_half,
                                pltpu.roll(ys, LANES - HEAD_DIM // 2, 1),
                                pltpu.roll(ys, HEAD_DIM // 2, 1))
            ys = ys * cos + partner * sin
        if dup_heads:
            swapped = pltpu.roll(ys, HEAD_DIM, 1)
            out_ref[:, 2 * s * LANES:(2 * s + 1) * LANES] = jnp.where(head0, ys, swapped).astype(out_ref.dtype)
            out_ref[:, (2 * s + 1) * LANES:(2 * s + 2) * LANES] = jnp.where(head0, swapped, ys).astype(out_ref.dtype)
        else:
            out_ref[:, s * LANES:(s + 1) * LANES] = ys.astype(out_ref.dtype)


def _proj(h, g, w, cos, sin, rot_slabs, name, dup_heads=False):
    n, d = h.shape
    m = w.shape[1] * (HEADS_PER_SLAB if dup_heads else 1)
    seq = cos.shape[0]
    tile = min(seq, 1 << ((PROJ_TILE_ELEMS // m).bit_length() - 1))
    tiles_per_seq = seq // tile
    table = pl.BlockSpec((tile, LANES), lambda i: (i % tiles_per_seq, 0))
    return pl.pallas_call(
        functools.partial(_proj_kernel, rot_slabs=rot_slabs, dup_heads=dup_heads),
        out_shape=jax.ShapeDtypeStruct((n, m), BF16),
        grid=(n // tile,),
        in_specs=[pl.BlockSpec((tile, d), lambda i: (i, 0)), _resident((1, d)), _resident(w.shape),
                  table, table],
        out_specs=pl.BlockSpec((tile, m), lambda i: (i, 0)),
        compiler_params=_params(("parallel",)),
        name=name,
    )(h, g.reshape(1, d), w, cos, sin)


def _rotary_tables(seq):
    half = HEAD_DIM // 2
    inv_freq = ROPE_THETA ** (-jnp.arange(half, dtype=F32) / half)
    ang = jnp.arange(seq, dtype=F32)[:, None] * inv_freq[None, :]
    cos, sin = jnp.cos(ang), jnp.sin(ang)
    cos_head = jnp.concatenate([cos, cos], axis=-1)
    sin_head = jnp.concatenate([-sin, sin], axis=-1)
    return jnp.tile(cos_head, (1, HEADS_PER_SLAB)), jnp.tile(sin_head, (1, HEADS_PER_SLAB))


def _sb_kernel(*refs, n_cast):
    q_ref, k_ref, v_ref, tri_ref = refs[:4]
    cast_in = refs[4:4 + n_cast]
    o_ref = refs[4 + n_cast]
    cast_out = refs[5 + n_cast:5 + 2 * n_cast]
    mask_ref, lb_ref, split_ref, acc_ref, carry_ref, gqs_ref, gacc_ref, gcarry_ref = refs[5 + 2 * n_cast:]
    for w_ref, w_out_ref in zip(cast_in, cast_out):
        w_out_ref[...] = w_ref[...].astype(w_out_ref.dtype)

    t, band, ck, u_tiles = SB_TILE, SB_BAND, SB_CHUNK, SB_INTERLEAVE
    nck = band // ck
    seq = q_ref.shape[0]
    q_tiles = seq // t
    head_tiles = band // t
    band_tiles = q_tiles - head_tiles
    sizes = [u_tiles] * (band_tiles // u_tiles) + ([band_tiles % u_tiles] if band_tiles % u_tiles else [])
    lane = lax.broadcasted_iota(jnp.int32, (t, LANES), 1)
    head0 = lane < HEAD_DIM
    qrow = lax.broadcasted_iota(jnp.int32, (2 * t, ck), 0) & (t - 1)
    col = lax.broadcasted_iota(jnp.int32, (2 * t, ck), 1)
    mask_ref[...] = jnp.where(col < qrow + (band - t - (nck - 1) * ck), 1.0, 0.0)

    def aligned(offset):
        return offset if isinstance(offset, int) else pl.multiple_of(offset, t)

    def stacked_queries(i):
        q = q_ref[pl.ds(aligned(i * t), t), :] * ATTN_SCALE
        zero = jnp.zeros_like(q)
        return jnp.concatenate([jnp.where(head0, q, zero), jnp.where(head0, zero, q)], axis=0)

    def key_rows(start):
        return pl.ds(aligned(start), band)

    def logits(qs, start):
        z = lax.dot_general(qs, k_ref[key_rows(start), :], (((1,), (1,)), ((), ())),
                            preferred_element_type=F32)
        sp = jnp.log(1.0 + jnp.exp2(jnp.abs(z) * -LOG2E))
        log_beta = jnp.minimum(z, 0.0) - sp
        return log_beta, log_beta - z

    def split_chunk(l):
        hi = l.astype(BF16)
        lo = (l - hi.astype(F32)).astype(BF16)
        return jnp.concatenate([hi, lo], axis=1)

    def suffix_sums(split):
        return jnp.dot(split, tri_ref[...], preferred_element_type=F32)

    def chunk(x, c):
        return x[:, c * ck:(c + 1) * ck]

    def write_output(i, acc):
        o_ref[pl.ds(aligned(i * t), t), :] = jnp.where(head0, acc[:t], acc[t:]).astype(o_ref.dtype)

    def band_start(i):
        return (i - (head_tiles - 1)) * t

    def sweep_start(i, head):
        return 0 if head else band_start(i)

    def causal(x, c, i, head):
        if head:
            return jnp.where(col + c * ck < qrow + i * t, x, 0.0)
        return x * mask_ref[...] if c == nck - 1 else x

    def logits_stage(parity, slot, i, head=False):
        log_beta, log_1m = logits(stacked_queries(i), sweep_start(i, head))
        lb_ref[parity, slot] = log_beta
        for c in range(nck):
            split_ref[parity, slot, c * 2 * t:(c + 1) * 2 * t, :] = split_chunk(causal(chunk(log_1m, c), c, i, head))

    def weights_stage(parity, slot, i, sums, head=False):
        log_beta = lb_ref[parity, slot]
        later = None
        ws = [None] * nck
        for c in reversed(range(nck)):
            sums_c = sums[c * 2 * t:(c + 1) * 2 * t]
            x = chunk(log_beta, c) + chunk(sums_c, 0)
            if later is not None:
                x = x + later
            ws[c] = causal(jnp.exp(x), c, i, head).astype(BF16)
            later = chunk(sums_c, 1) if later is None else later + chunk(sums_c, 1)
        acc = jnp.dot(jnp.concatenate(ws, axis=1), v_ref[key_rows(sweep_start(i, head)), :],
                      preferred_element_type=F32)
        acc_ref[parity, slot] = acc
        carry_ref[parity, slot] = later
        write_output(i, acc)
        return later

    def generic_sweep(i, start, limit):
        log_beta, log_1m = logits(gqs_ref[...], start)
        bound = jnp.minimum(i * t + qrow, limit) - start
        valids = [col + c * ck < bound for c in range(nck)]
        sums = suffix_sums(jnp.concatenate(
            [split_chunk(jnp.where(valids[c], chunk(log_1m, c), 0.0)) for c in range(nck)], axis=0))
        later = gcarry_ref[...]
        ws = [None] * nck
        for c in reversed(range(nck)):
            sums_c = sums[c * 2 * t:(c + 1) * 2 * t]
            x = chunk(log_beta, c) + chunk(sums_c, 0) + later
            ws[c] = jnp.where(valids[c], jnp.exp(x), 0.0).astype(BF16)
            later = later + chunk(sums_c, 1)
        gacc_ref[...] += jnp.dot(jnp.concatenate(ws, axis=1), v_ref[key_rows(start), :],
                                 preferred_element_type=F32)
        gcarry_ref[...] = later
        return jnp.max(later)

    def continue_tile(parity, slot, i):
        gqs_ref[...] = stacked_queries(i)
        gacc_ref[...] = acc_ref[parity, slot]
        gcarry_ref[...] = carry_ref[parity, slot]

        def cond(state):
            limit, top = state
            return jnp.logical_and(limit > 0, top > F32_EXP_UNDERFLOW)

        def body(state):
            limit, _ = state
            start = jnp.maximum(limit - band, 0)
            return start, generic_sweep(i, start, limit)

        lax.while_loop(cond, body, (band_start(i), jnp.max(gcarry_ref[...])))
        write_output(i, gacc_ref[...])

    def first_tile(g):
        return head_tiles + (sum(sizes[:g]) if isinstance(g, int) else g * u_tiles)

    def stage(parity, group, size, next_group, next_size):
        sums = [suffix_sums(split_ref[parity, slot]) for slot in range(size)]
        if next_group is not None:
            for slot in range(next_size):
                logits_stage(1 - parity, slot, first_tile(next_group) + slot)
        top = None
        for slot in range(size):
            carry = weights_stage(parity, slot, first_tile(group) + slot, sums[slot])
            top = carry if top is None else jnp.maximum(top, carry)

        @pl.when(jnp.max(top) > F32_EXP_UNDERFLOW)
        def _():
            def one(slot, c):
                continue_tile(parity, slot, first_tile(group) + slot)
                return c
            lax.fori_loop(0, size, one, 0)

    for i in range(head_tiles):
        logits_stage(1, i, i, head=True)
    for i in range(head_tiles):
        weights_stage(1, i, i, suffix_sums(split_ref[1, i]), head=True)
    for slot in range(sizes[0]):
        logits_stage(0, slot, first_tile(0) + slot)

    def stage_pair(m, c):
        stage(0, 2 * m, u_tiles, 2 * m + 1, u_tiles)
        stage(1, 2 * m + 1, u_tiles, 2 * m + 2, u_tiles)
        return c

    full_groups = band_tiles // u_tiles
    pairs = max(full_groups - 1, 0) // 2
    lax.fori_loop(0, pairs, stage_pair, 0)
    for g in range(2 * pairs, len(sizes)):
        has_next = g + 1 < len(sizes)
        stage(g % 2, g, sizes[g], g + 1 if has_next else None, sizes[g + 1] if has_next else 0)


def _suffix_matrix(n):
    row = lax.broadcasted_iota(jnp.int32, (n, n), 0)
    col = lax.broadcasted_iota(jnp.int32, (n, n), 1)
    half = jnp.concatenate([(row > col).astype(BF16), jnp.ones((n, n), BF16)], axis=1)
    return jnp.concatenate([half, half], axis=0)


def _sb_attention(qkv, batch, seq, weights_to_cast):
    n = qkv.shape[0]
    slabs = SB_HEADS // HEADS_PER_SLAB
    steps = batch * slabs
    tri = _suffix_matrix(SB_CHUNK)
    rows, nck = 2 * SB_TILE, SB_BAND // SB_CHUNK
    assert seq % SB_TILE == 0 and seq >= SB_BAND and SB_CHUNK == LANES and SB_BAND // SB_TILE <= SB_INTERLEAVE
    cast_in_specs, cast_out_specs, cast_shapes = [], [], []
    for w, layer in weights_to_cast:
        w_rows, w_cols = w.shape[-2:]
        block = w_rows // steps
        assert w_rows % steps == 0 and block % BF16_SUBLANES == 0
        if w.ndim == 2:
            cast_in_specs.append(pl.BlockSpec((block, w_cols), lambda b, s: (b * slabs + s, 0)))
        else:
            cast_in_specs.append(pl.BlockSpec((None, block, w_cols), lambda b, s, layer=layer: (layer, b * slabs + s, 0)))
        cast_out_specs.append(pl.BlockSpec((block, w_cols), lambda b, s: (b * slabs + s, 0)))
        cast_shapes.append(jax.ShapeDtypeStruct((w_rows, w_cols), BF16))
    o, *cast = pl.pallas_call(
        functools.partial(_sb_kernel, n_cast=len(weights_to_cast)),
        out_shape=[jax.ShapeDtypeStruct((n, SB_HEADS * HEAD_DIM), BF16)] + cast_shapes,
        grid=(batch, slabs),
        in_specs=[
            pl.BlockSpec((seq, LANES), lambda b, s: (b, s)),
            pl.BlockSpec((seq, LANES), lambda b, s: (b, slabs + s)),
            pl.BlockSpec((seq, LANES), lambda b, s: (b, 2 * slabs + s)),
            _resident(tri.shape),
        ] + cast_in_specs,
        out_specs=[pl.BlockSpec((seq, LANES), lambda b, s: (b, s))] + cast_out_specs,
        scratch_shapes=[
            pltpu.VMEM((rows, SB_CHUNK), F32),
            pltpu.VMEM((2, SB_INTERLEAVE, rows, SB_BAND), F32),
            pltpu.VMEM((2, SB_INTERLEAVE, nck * rows, 2 * SB_CHUNK), BF16),
            pltpu.VMEM((2, SB_INTERLEAVE, rows, LANES), F32),
            pltpu.VMEM((2, SB_INTERLEAVE, rows, LANES), F32),
            pltpu.VMEM((rows, LANES), BF16),
            pltpu.VMEM((rows, LANES), F32),
            pltpu.VMEM((rows, LANES), F32),
        ],
        compiler_params=_params(("parallel", "parallel")),
        name="sb_attention",
    )(qkv, qkv, qkv, tri, *[w for w, _ in weights_to_cast])
    return o, cast


def _swa_kernel(sink_ref, bias_first_ref, bias_ref, q_ref, kvp_ref, kvc_ref, o_ref):
    w = WINDOW
    lane = lax.broadcasted_iota(jnp.int32, (w, LANES), 1)
    head0 = lane < HEAD_DIM
    zero = jnp.zeros((w, LANES), BF16)
    ones = jnp.ones((2 * w, LANES), BF16)
    for blk in range(SWA_BLOCKS_PER_STEP):
        rows = slice(blk * w, (blk + 1) * w)
        bias = jnp.concatenate([(bias_first_ref if blk == 0 else bias_ref)[0]] * SWA_GROUP, axis=0)
        for h in range(SWA_KV_HEADS):
            k_cols = slice(h * LANES, (h + 1) * LANES)
            v_cols = slice((SWA_KV_HEADS + h) * LANES, (SWA_KV_HEADS + h + 1) * LANES)
            if blk == 0:
                prev_k, prev_v = kvp_ref[:, k_cols], kvp_ref[:, v_cols]
            else:
                prev_k, prev_v = kvc_ref[(blk - 1) * w:blk * w, k_cols], kvc_ref[(blk - 1) * w:blk * w, v_cols]
            kk = jnp.concatenate([prev_k, kvc_ref[rows, k_cols]], axis=0)
            vv = jnp.concatenate([prev_v, kvc_ref[rows, v_cols]], axis=0)
            vv_ones = jnp.concatenate([vv, ones], axis=1)
            stacked, sinks = [], []
            for g in range(SWA_GROUP):
                head = h * SWA_GROUP + g
                slab = head // HEADS_PER_SLAB
                q = q_ref[rows, slab * LANES:(slab + 1) * LANES] * ATTN_SCALE
                stacked.append(jnp.where(head0, q, zero) if head % HEADS_PER_SLAB == 0 else jnp.where(head0, zero, q))
                sinks.append(jnp.full((w, LANES), sink_ref[head], F32))
            qs = jnp.concatenate(stacked, axis=0)
            sink = jnp.concatenate(sinks, axis=0)
            s = lax.dot_general(qs, kk, (((1,), (1,)), ((), ())), preferred_element_type=F32) + bias
            m = jnp.maximum(jnp.broadcast_to(jnp.max(s, axis=1, keepdims=True), sink.shape), sink)
            e = jnp.exp(s - jnp.concatenate([m, m], axis=1)).astype(BF16)
            ov = jnp.dot(e, vv_ones, preferred_element_type=F32)
            for pair in range(SWA_GROUP // HEADS_PER_SLAB):
                slab = (h * SWA_GROUP) // HEADS_PER_SLAB + pair
                lo, hi = slice(2 * pair * w, (2 * pair + 1) * w), slice((2 * pair + 1) * w, (2 * pair + 2) * w)
                both = lambda x: jnp.where(head0, x[lo], x[hi])
                denom = both(ov[:, LANES:]) + jnp.exp(both(sink) - both(m))
                o_ref[rows, slab * LANES:(slab + 1) * LANES] = (both(ov[:, :LANES]) * (1.0 / denom)).astype(o_ref.dtype)


def _swa_attention(q, kv, sinks, batch, seq):
    n, dq = q.shape
    step_rows = SWA_BLOCKS_PER_STEP * WINDOW
    steps = seq // step_rows
    dkv = kv.shape[1]
    qi = lax.broadcasted_iota(jnp.int32, (WINDOW, 2 * WINDOW), 0)
    ki = lax.broadcasted_iota(jnp.int32, (WINDOW, 2 * WINDOW), 1)
    diff = qi + WINDOW - ki
    in_window = (diff >= 0) & (diff < WINDOW)
    bias = jnp.where(jnp.stack([in_window & (ki >= WINDOW), in_window]), 0.0, -jnp.inf).astype(F32)
    bias_spec = lambda index: pl.BlockSpec((1, WINDOW, 2 * WINDOW), index)
    return pl.pallas_call(
        _swa_kernel,
        out_shape=jax.ShapeDtypeStruct((n, dq), BF16),
        grid=(batch, steps),
        in_specs=[
            pl.BlockSpec(memory_space=pltpu.SMEM),
            bias_spec(lambda b, i: (jnp.minimum(i, 1), 0, 0)),
            bias_spec(lambda b, i: (1, 0, 0)),
            pl.BlockSpec((step_rows, dq), lambda b, i: (b * steps + i, 0)),
            pl.BlockSpec((WINDOW, dkv),
                         lambda b, i: ((b * steps + i) * SWA_BLOCKS_PER_STEP - jnp.minimum(i, 1), 0)),
            pl.BlockSpec((step_rows, dkv), lambda b, i: (b * steps + i, 0)),
        ],
        out_specs=pl.BlockSpec((step_rows, dq), lambda b, i: (b * steps + i, 0)),
        compiler_params=_params(("parallel", "arbitrary")),
        name="swa_attention",
    )(sinks, bias, bias, q, kv, kv)


def kernel(x, ffn1_norm, ffn1_w_in, ffn1_w_out, mix_norm, ffn2_norm, ffn2_w_in, ffn2_w_out,
           sb_w_qkv, sb_w_o, kv_norm, kv_w, swa_w_q, swa_sinks, swa_w_o, final_norm):
    batch, seq, d = x.shape
    n = batch * seq
    cos, sin = _rotary_tables(seq)
    bf = _to_bf16
    h = x.reshape(n, d)

    h, w_qkv = _ffn(h, ffn1_norm[0], bf(ffn1_w_in, 0), bf(ffn1_w_out, 0), side_cast=(sb_w_qkv, 0), name="ffn1_l0")
    qkv = _proj(h, mix_norm[0], w_qkv, cos, sin, rot_slabs=0, name="proj_qkv")
    later_weights = [(ffn2_w_in, 0), (ffn2_w_out, 0), (sb_w_o, 0), (kv_w, None), (ffn1_w_in, 1), (ffn1_w_out, 1),
                     (swa_w_q, 0), (ffn2_w_in, 1), (ffn2_w_out, 1), (swa_w_o, 0)]
    o, (w2_in0, w2_out0, w_o0, w_kv, w1_in1, w1_out1, w_q1, w2_in1, w2_out1, w_o1) = _sb_attention(
        qkv, batch, seq, later_weights)
    h = _ffn(h, ffn2_norm[0], w2_in0, w2_out0, attn=(o, w_o0), name="ffn2_l0")
    k_slabs = SWA_KV_HEADS * HEAD_DIM // LANES
    kv = _proj(h, kv_norm, w_kv, cos, sin, rot_slabs=k_slabs, name="proj_kv", dup_heads=True)

    h = _ffn(h, ffn1_norm[1], w1_in1, w1_out1, name="ffn1_l1")
    q = _proj(h, mix_norm[1], w_q1, cos, sin, rot_slabs=SWA_Q_HEADS * HEAD_DIM // LANES, name="proj_q")
    o = _swa_attention(q, kv, swa_sinks[0], batch, seq)
    h = _ffn(h, ffn2_norm[1], w2_in1, w2_out1, attn=(o, w_o1), final_g=final_norm, name="ffn2_l1")
    return h.reshape(batch, seq, d)
```

```python
import functools

import jax
import jax.numpy as jnp
from jax import lax
from jax.experimental import pallas as pl
from jax.experimental.pallas import tpu as pltpu

F32 = jnp.float32
BF16 = jnp.bfloat16

HEAD_DIM = 64
SB_HEADS = 16
SWA_Q_HEADS = 16
SWA_KV_HEADS = 4
SWA_GROUP = SWA_Q_HEADS // SWA_KV_HEADS
WINDOW = 128
D_FF = 2816
ROPE_THETA = 10000.0
RMS_EPS = 1e-6
FFN_RES_SCALE = 0.5
ATTN_SCALE = HEAD_DIM ** -0.5
LOG2E = 1.4426950408889634

LANES = 128
BF16_SUBLANES = 16
HEADS_PER_SLAB = LANES // HEAD_DIM
VMEM_LIMIT_BYTES = 56 * 1024 * 1024

PROJ_TILE_ELEMS = 3 * 1024 * 1024
PROJ_CHUNK = 256
CAST_BLOCK_BYTES = 4 * 1024 * 1024
FFN_TILE = 1024
FF_CHUNKS = 11
SWA_BLOCKS_PER_STEP = 8
SB_TILE = 64
SB_BAND = 256
SB_CHUNK = 128
SB_INTERLEAVE = 16
F32_EXP_UNDERFLOW = -105.0


def _rms(x, g):
    return x * lax.rsqrt(jnp.mean(x * x, axis=-1, keepdims=True) + RMS_EPS) * g


def _params(semantics):
    return pltpu.CompilerParams(dimension_semantics=semantics, vmem_limit_bytes=VMEM_LIMIT_BYTES)


def _resident(shape):
    return pl.BlockSpec(shape, lambda *_: (0,) * len(shape), pipeline_mode=pl.Buffered(1))


def _cast_kernel(w_ref, out_ref):
    out_ref[...] = w_ref[...].astype(out_ref.dtype)


def _to_bf16(w, layer=None, name="cast"):
    rows, cols = w.shape[-2:]
    block_rows = max(BF16_SUBLANES, min(rows, CAST_BLOCK_BYTES // (4 * cols) // BF16_SUBLANES * BF16_SUBLANES))
    while rows % block_rows:
        block_rows -= BF16_SUBLANES
    if layer is None:
        in_spec = pl.BlockSpec((block_rows, cols), lambda i: (i, 0))
    else:
        in_spec = pl.BlockSpec((None, block_rows, cols), lambda i: (layer, i, 0))
    return pl.pallas_call(
        _cast_kernel,
        out_shape=jax.ShapeDtypeStruct((rows, cols), BF16),
        grid=(rows // block_rows,),
        in_specs=[in_spec],
        out_specs=pl.BlockSpec((block_rows, cols), lambda i: (i, 0)),
        compiler_params=_params(("parallel",)),
        name=name,
    )(w)


def _ffn_kernel(*refs, has_attn, has_final_norm, has_side_cast):
    refs = list(refs)
    h_ref = refs.pop(0)
    if has_attn:
        o_ref, wo_ref = refs.pop(0), refs.pop(0)
    g_ref, win_ref, wout_ref = refs.pop(0), refs.pop(0), refs.pop(0)
    if has_final_norm:
        fg_ref = refs.pop(0)
    if has_side_cast:
        side_ref, out_ref, side_out_ref = refs
        side_out_ref[...] = side_ref[...].astype(side_out_ref.dtype)
    else:
        (out_ref,) = refs

    x = h_ref[...]
    if has_attn:
        x = x + jnp.dot(o_ref[...], wo_ref[...], preferred_element_type=F32)
    xn = _rms(x, g_ref[...]).astype(BF16)
    fc = D_FF // FF_CHUNKS
    y = None
    for c in range(FF_CHUNKS):
        gate = jnp.dot(xn, win_ref[:, c * fc:(c + 1) * fc], preferred_element_type=F32)
        up = jnp.dot(xn, win_ref[:, D_FF + c * fc:D_FF + (c + 1) * fc], preferred_element_type=F32)
        act = (gate * jax.nn.sigmoid(gate) * up).astype(BF16)
        yc = jnp.dot(act, wout_ref[c * fc:(c + 1) * fc, :].astype(BF16), preferred_element_type=F32)
        y = yc if y is None else y + yc
    x = x + FFN_RES_SCALE * y
    if has_final_norm:
        x = _rms(x, fg_ref[...])
    out_ref[...] = x


def _ffn(h, g, w_in, w_out, attn=None, final_g=None, side_cast=None, name="ffn"):
    n, d = h.shape
    steps = n // FFN_TILE
    row = pl.BlockSpec((FFN_TILE, d), lambda i: (i, 0))
    args, specs = [h], [row]
    if attn is not None:
        o, w_o = attn
        args += [o, w_o]
        specs += [row, _resident(w_o.shape)]
    if isinstance(w_out, tuple):
        w_out, w_out_layer = w_out
        w_out_spec = pl.BlockSpec((None,) + w_out.shape[1:], lambda i: (w_out_layer, 0, 0),
                                  pipeline_mode=pl.Buffered(1))
    else:
        w_out_spec = _resident(w_out.shape)
    args += [g.reshape(1, d), w_in, w_out]
    specs += [_resident((1, d)), _resident(w_in.shape), w_out_spec]
    if final_g is not None:
        args.append(final_g.reshape(1, d))
        specs.append(_resident((1, d)))
    out_shape, out_specs = jax.ShapeDtypeStruct((n, d), F32), row
    if side_cast is not None:
        w, layer = side_cast
        w_rows, w_cols = w.shape[-2:]
        block = w_rows // steps
        assert w_rows % steps == 0 and block % BF16_SUBLANES == 0
        args.append(w)
        specs.append(pl.BlockSpec((None, block, w_cols), lambda i: (layer, i, 0)))
        out_shape = [out_shape, jax.ShapeDtypeStruct((w_rows, w_cols), BF16)]
        out_specs = [row, pl.BlockSpec((block, w_cols), lambda i: (i, 0))]
    return pl.pallas_call(
        functools.partial(_ffn_kernel, has_attn=attn is not None, has_final_norm=final_g is not None,
                          has_side_cast=side_cast is not None),
        out_shape=out_shape,
        grid=(steps,),
        in_specs=specs,
        out_specs=out_specs,
        compiler_params=_params(("parallel",)),
        name=name,
    )(*args)


def _proj_kernel(h_ref, g_ref, w_ref, cos_ref, sin_ref, out_ref, *, rot_slabs, dup_heads):
    xn = _rms(h_ref[...], g_ref[...]).astype(BF16)
    cos, sin = cos_ref[...], sin_ref[...]
    lane = lax.broadcasted_iota(jnp.int32, cos.shape, 1)
    first_half = (lane % HEAD_DIM) < (HEAD_DIM // 2)
    head0 = lane < HEAD_DIM
    slabs_per_chunk = PROJ_CHUNK // LANES
    for s in range(w_ref.shape[1] // LANES):
        if s % slabs_per_chunk == 0:
            y = jnp.dot(xn, w_ref[:, s * LANES:(s + slabs_per_chunk) * LANES], preferred_element_type=F32)
        ys = y[:, (s % slabs_per_chunk) * LANES:(s % slabs_per_chunk + 1) * LANES]
        if s < rot_slabs:
            partner = jnp.where(first_half,
                                pltpu.roll(ys, LANES - HEAD_DIM // 2, 1),
                                pltpu.roll(ys, HEAD_DIM // 2, 1))
            ys = ys * cos + partner * sin
        if dup_heads:
            swapped = pltpu.roll(ys, HEAD_DIM, 1)
            out_ref[:, 2 * s * LANES:(2 * s + 1) * LANES] = jnp.where(head0, ys, swapped).astype(out_ref.dtype)
            out_ref[:, (2 * s + 1) * LANES:(2 * s + 2) * LANES] = jnp.where(head0, swapped, ys).astype(out_ref.dtype)
        else:
            out_ref[:, s * LANES:(s + 1) * LANES] = ys.astype(out_ref.dtype)


def _proj(h, g, w, cos, sin, rot_slabs, name, dup_heads=False):
    n, d = h.shape
    m = w.shape[1] * (HEADS_PER_SLAB if dup_heads else 1)
    seq = cos.shape[0]
    tile = min(seq, 1 << ((PROJ_TILE_ELEMS // m).bit_length() - 1))
    tiles_per_seq = seq // tile
    table = pl.BlockSpec((tile, LANES), lambda i: (i % tiles_per_seq, 0))
    return pl.pallas_call(
        functools.partial(_proj_kernel, rot_slabs=rot_slabs, dup_heads=dup_heads),
        out_shape=jax.ShapeDtypeStruct((n, m), BF16),
        grid=(n // tile,),
        in_specs=[pl.BlockSpec((tile, d), lambda i: (i, 0)), _resident((1, d)), _resident(w.shape),
                  table, table],
        out_specs=pl.BlockSpec((tile, m), lambda i: (i, 0)),
        compiler_params=_params(("parallel",)),
        name=name,
    )(h, g.reshape(1, d), w, cos, sin)


def _rotary_tables(seq):
    half = HEAD_DIM // 2
    inv_freq = ROPE_THETA ** (-jnp.arange(half, dtype=F32) / half)
    ang = jnp.arange(seq, dtype=F32)[:, None] * inv_freq[None, :]
    cos, sin = jnp.cos(ang), jnp.sin(ang)
    cos_head = jnp.concatenate([cos, cos], axis=-1)
    sin_head = jnp.concatenate([-sin, sin], axis=-1)
    return jnp.tile(cos_head, (1, HEADS_PER_SLAB)), jnp.tile(sin_head, (1, HEADS_PER_SLAB))


def _sb_kernel(*refs, n_cast):
    q_ref, k_ref, v_ref, tri_ref = refs[:4]
    cast_in = refs[4:4 + n_cast]
    o_ref = refs[4 + n_cast]
    cast_out = refs[5 + n_cast:5 + 2 * n_cast]
    mask_ref, lb_ref, split_ref, acc_ref, carry_ref, gqs_ref, gacc_ref, gcarry_ref = refs[5 + 2 * n_cast:]
    for w_ref, w_out_ref in zip(cast_in, cast_out):
        w_out_ref[...] = w_ref[...].astype(w_out_ref.dtype)

    t, band, ck, u_tiles = SB_TILE, SB_BAND, SB_CHUNK, SB_INTERLEAVE
    nck = band // ck
    seq = q_ref.shape[0]
    q_tiles = seq // t
    head_tiles = band // t
    band_tiles = q_tiles - head_tiles
    sizes = [u_tiles] * (band_tiles // u_tiles) + ([band_tiles % u_tiles] if band_tiles % u_tiles else [])
    lane = lax.broadcasted_iota(jnp.int32, (t, LANES), 1)
    head0 = lane < HEAD_DIM
    qrow = lax.broadcasted_iota(jnp.int32, (2 * t, ck), 0) & (t - 1)
    col = lax.broadcasted_iota(jnp.int32, (2 * t, ck), 1)
    mask_ref[...] = jnp.where(col < qrow + (band - t - (nck - 1) * ck), 1.0, 0.0)

    def aligned(offset):
        return offset if isinstance(offset, int) else pl.multiple_of(offset, t)

    def stacked_queries(i):
        q = q_ref[pl.ds(aligned(i * t), t), :] * ATTN_SCALE
        zero = jnp.zeros_like(q)
        return jnp.concatenate([jnp.where(head0, q, zero), jnp.where(head0, zero, q)], axis=0)

    def key_rows(start):
        return pl.ds(aligned(start), band)

    def logits(qs, start):
        z = lax.dot_general(qs, k_ref[key_rows(start), :], (((1,), (1,)), ((), ())),
                            preferred_element_type=F32)
        sp = jnp.log(1.0 + jnp.exp2(jnp.abs(z) * -LOG2E))
        log_beta = jnp.minimum(z, 0.0) - sp
        return log_beta, log_beta - z

    def split_chunk(l):
        hi = l.astype(BF16)
        lo = (l - hi.astype(F32)).astype(BF16)
        return jnp.concatenate([hi, lo], axis=1)

    def suffix_sums(split):
        return jnp.dot(split, tri_ref[...], preferred_element_type=F32)

    def chunk(x, c):
        return x[:, c * ck:(c + 1) * ck]

    def write_output(i, acc):
        o_ref[pl.ds(aligned(i * t), t), :] = jnp.where(head0, acc[:t], acc[t:]).astype(o_ref.dtype)

    def band_start(i):
        return (i - (head_tiles - 1)) * t

    def sweep_start(i, head):
        return 0 if head else band_start(i)

    def causal(x, c, i, head):
        if head:
            return jnp.where(col + c * ck < qrow + i * t, x, 0.0)
        return x * mask_ref[...] if c == nck - 1 else x

    def logits_stage(parity, slot, i, head=False):
        log_beta, log_1m = logits(stacked_queries(i), sweep_start(i, head))
        lb_ref[parity, slot] = log_beta
        for c in range(nck):
            split_ref[parity, slot, c * 2 * t:(c + 1) * 2 * t, :] = split_chunk(causal(chunk(log_1m, c), c, i, head))

    def weights_stage(parity, slot, i, sums, head=False):
        log_beta = lb_ref[parity, slot]
        later = None
        ws = [None] * nck
        for c in reversed(range(nck)):
            sums_c = sums[c * 2 * t:(c + 1) * 2 * t]
            x = chunk(log_beta, c) + chunk(sums_c, 0)
            if later is not None:
                x = x + later
            ws[c] = causal(jnp.exp(x), c, i, head).astype(BF16)
            later = chunk(sums_c, 1) if later is None else later + chunk(sums_c, 1)
        acc = jnp.dot(jnp.concatenate(ws, axis=1), v_ref[key_rows(sweep_start(i, head)), :],
                      preferred_element_type=F32)
        acc_ref[parity, slot] = acc
        carry_ref[parity, slot] = later
        write_output(i, acc)
        return later

    def generic_sweep(i, start, limit):
        log_beta, log_1m = logits(gqs_ref[...], start)
        bound = jnp.minimum(i * t + qrow, limit) - start
        valids = [col + c * ck < bound for c in range(nck)]
        sums = suffix_sums(jnp.concatenate(
            [split_chunk(jnp.where(valids[c], chunk(log_1m, c), 0.0)) for c in range(nck)], axis=0))
        later = gcarry_ref[...]
        ws = [None] * nck
        for c in reversed(range(nck)):
            sums_c = sums[c * 2 * t:(c + 1) * 2 * t]
            x = chunk(log_beta, c) + chunk(sums_c, 0) + later
            ws[c] = jnp.where(valids[c], jnp.exp(x), 0.0).astype(BF16)
            later = later + chunk(sums_c, 1)
        gacc_ref[...] += jnp.dot(jnp.concatenate(ws, axis=1), v_ref[key_rows(start), :],
                                 preferred_element_type=F32)
        gcarry_ref[...] = later
        return jnp.max(later)

    def continue_tile(parity, slot, i):
        gqs_ref[...] = stacked_queries(i)
        gacc_ref[...] = acc_ref[parity, slot]
        gcarry_ref[...] = carry_ref[parity, slot]

        def cond(state):
            limit, top = state
            return jnp.logical_and(limit > 0, top > F32_EXP_UNDERFLOW)

        def body(state):
            limit, _ = state
            start = jnp.maximum(limit - band, 0)
            return start, generic_sweep(i, start, limit)

        lax.while_loop(cond, body, (band_start(i), jnp.max(gcarry_ref[...])))
        write_output(i, gacc_ref[...])

    def first_tile(g):
        return head_tiles + (sum(sizes[:g]) if isinstance(g, int) else g * u_tiles)

    def stage(parity, group, size, next_group, next_size):
        sums = [suffix_sums(split_ref[parity, slot]) for slot in range(size)]
        if next_group is not None:
            for slot in range(next_size):
                logits_stage(1 - parity, slot, first_tile(next_group) + slot)
        top = None
        for slot in range(size):
            carry = weights_stage(parity, slot, first_tile(group) + slot, sums[slot])
            top = carry if top is None else jnp.maximum(top, carry)

        @pl.when(jnp.max(top) > F32_EXP_UNDERFLOW)
        def _():
            def one(slot, c):
                continue_tile(parity, slot, first_tile(group) + slot)
                return c
            lax.fori_loop(0, size, one, 0)

    for i in range(head_tiles):
        logits_stage(1, i, i, head=True)
    for i in range(head_tiles):
        weights_stage(1, i, i, suffix_sums(split_ref[1, i]), head=True)
    for slot in range(sizes[0]):
        logits_stage(0, slot, first_tile(0) + slot)

    def stage_pair(m, c):
        stage(0, 2 * m, u_tiles, 2 * m + 1, u_tiles)
        stage(1, 2 * m + 1, u_tiles, 2 * m + 2, u_tiles)
        return c

    full_groups = band_tiles // u_tiles
    pairs = max(full_groups - 1, 0) // 2
    lax.fori_loop(0, pairs, stage_pair, 0)
    for g in range(2 * pairs, len(sizes)):
        has_next = g + 1 < len(sizes)
        stage(g % 2, g, sizes[g], g + 1 if has_next else None, sizes[g + 1] if has_next else 0)


def _suffix_matrix(n):
    row = lax.broadcasted_iota(jnp.int32, (n, n), 0)
    col = lax.broadcasted_iota(jnp.int32, (n, n), 1)
    half = jnp.concatenate([(row > col).astype(BF16), jnp.ones((n, n), BF16)], axis=1)
    return jnp.concatenate([half, half], axis=0)


def _sb_attention(qkv, batch, seq, weights_to_cast):
    n = qkv.shape[0]
    slabs = SB_HEADS // HEADS_PER_SLAB
    steps = batch * slabs
    tri = _suffix_matrix(SB_CHUNK)
    rows, nck = 2 * SB_TILE, SB_BAND // SB_CHUNK
    assert seq % SB_TILE == 0 and seq >= SB_BAND and SB_CHUNK == LANES and SB_BAND // SB_TILE <= SB_INTERLEAVE
    cast_in_specs, cast_out_specs, cast_shapes = [], [], []
    for w, layer in weights_to_cast:
        w_rows, w_cols = w.shape[-2:]
        block = w_rows // steps
        assert w_rows % steps == 0 and block % BF16_SUBLANES == 0
        if w.ndim == 2:
            cast_in_specs.append(pl.BlockSpec((block, w_cols), lambda b, s: (b * slabs + s, 0)))
        else:
            cast_in_specs.append(pl.BlockSpec((None, block, w_cols), lambda b, s, layer=layer: (layer, b * slabs + s, 0)))
        cast_out_specs.append(pl.BlockSpec((block, w_cols), lambda b, s: (b * slabs + s, 0)))
        cast_shapes.append(jax.ShapeDtypeStruct((w_rows, w_cols), BF16))
    o, *cast = pl.pallas_call(
        functools.partial(_sb_kernel, n_cast=len(weights_to_cast)),
        out_shape=[jax.ShapeDtypeStruct((n, SB_HEADS * HEAD_DIM), BF16)] + cast_shapes,
        grid=(batch, slabs),
        in_specs=[
            pl.BlockSpec((seq, LANES), lambda b, s: (b, s)),
            pl.BlockSpec((seq, LANES), lambda b, s: (b, slabs + s)),
            pl.BlockSpec((seq, LANES), lambda b, s: (b, 2 * slabs + s)),
            _resident(tri.shape),
        ] + cast_in_specs,
        out_specs=[pl.BlockSpec((seq, LANES), lambda b, s: (b, s))] + cast_out_specs,
        scratch_shapes=[
            pltpu.VMEM((rows, SB_CHUNK), F32),
            pltpu.VMEM((2, SB_INTERLEAVE, rows, SB_BAND), F32),
            pltpu.VMEM((2, SB_INTERLEAVE, nck * rows, 2 * SB_CHUNK), BF16),
            pltpu.VMEM((2, SB_INTERLEAVE, rows, LANES), F32),
            pltpu.VMEM((2, SB_INTERLEAVE, rows, LANES), F32),
            pltpu.VMEM((rows, LANES), BF16),
            pltpu.VMEM((rows, LANES), F32),
            pltpu.VMEM((rows, LANES), F32),
        ],
        compiler_params=_params(("parallel", "parallel")),
        name="sb_attention",
    )(qkv, qkv, qkv, tri, *[w for w, _ in weights_to_cast])
    return o, cast


def _swa_kernel(sink_ref, bias_first_ref, bias_ref, q_ref, kvp_ref, kvc_ref, o_ref):
    w = WINDOW
    lane = lax.broadcasted_iota(jnp.int32, (w, LANES), 1)
    head0 = lane < HEAD_DIM
    zero = jnp.zeros((w, LANES), BF16)
    ones = jnp.ones((2 * w, LANES), BF16)
    for blk in range(SWA_BLOCKS_PER_STEP):
        rows = slice(blk * w, (blk + 1) * w)
        bias = jnp.concatenate([(bias_first_ref if blk == 0 else bias_ref)[0]] * SWA_GROUP, axis=0)
        for h in range(SWA_KV_HEADS):
            k_cols = slice(h * LANES, (h + 1) * LANES)
            v_cols = slice((SWA_KV_HEADS + h) * LANES, (SWA_KV_HEADS + h + 1) * LANES)
            if blk == 0:
                prev_k, prev_v = kvp_ref[:, k_cols], kvp_ref[:, v_cols]
            else:
                prev_k, prev_v = kvc_ref[(blk - 1) * w:blk * w, k_cols], kvc_ref[(blk - 1) * w:blk * w, v_cols]
            kk = jnp.concatenate([prev_k, kvc_ref[rows, k_cols]], axis=0)
            vv = jnp.concatenate([prev_v, kvc_ref[rows, v_cols]], axis=0)
            vv_ones = jnp.concatenate([vv, ones], axis=1)
            stacked, sinks = [], []
            for g in range(SWA_GROUP):
                head = h * SWA_GROUP + g
                slab = head // HEADS_PER_SLAB
                q = q_ref[rows, slab * LANES:(slab + 1) * LANES] * ATTN_SCALE
                stacked.append(jnp.where(head0, q, zero) if head % HEADS_PER_SLAB == 0 else jnp.where(head0, zero, q))
                sinks.append(jnp.full((w, LANES), sink_ref[head], F32))
            qs = jnp.concatenate(stacked, axis=0)
            sink = jnp.concatenate(sinks, axis=0)
            s = lax.dot_general(qs, kk, (((1,), (1,)), ((), ())), preferred_element_type=F32) + bias
            m = jnp.maximum(jnp.broadcast_to(jnp.max(s, axis=1, keepdims=True), sink.shape), sink)
            e = jnp.exp(s - jnp.concatenate([m, m], axis=1)).astype(BF16)
            ov = jnp.dot(e, vv_ones, preferred_element_type=F32)
            for pair in range(SWA_GROUP // HEADS_PER_SLAB):
                slab = (h * SWA_GROUP) // HEADS_PER_SLAB + pair
                lo, hi = slice(2 * pair * w, (2 * pair + 1) * w), slice((2 * pair + 1) * w, (2 * pair + 2) * w)
                both = lambda x: jnp.where(head0, x[lo], x[hi])
                denom = both(ov[:, LANES:]) + jnp.exp(both(sink) - both(m))
                o_ref[rows, slab * LANES:(slab + 1) * LANES] = (both(ov[:, :LANES]) * (1.0 / denom)).astype(o_ref.dtype)


def _swa_attention(q, kv, sinks, batch, seq):
    n, dq = q.shape
    step_rows = SWA_BLOCKS_PER_STEP * WINDOW
    steps = seq // step_rows
    dkv = kv.shape[1]
    qi = lax.broadcasted_iota(jnp.int32, (WINDOW, 2 * WINDOW), 0)
    ki = lax.broadcasted_iota(jnp.int32, (WINDOW, 2 * WINDOW), 1)
    diff = qi + WINDOW - ki
    in_window = (diff >= 0) & (diff < WINDOW)
    bias = jnp.where(jnp.stack([in_window & (ki >= WINDOW), in_window]), 0.0, -jnp.inf).astype(F32)
    bias_spec = lambda index: pl.BlockSpec((1, WINDOW, 2 * WINDOW), index)
    return pl.pallas_call(
        _swa_kernel,
        out_shape=jax.ShapeDtypeStruct((n, dq), BF16),
        grid=(batch, steps),
        in_specs=[
            pl.BlockSpec(memory_space=pltpu.SMEM),
            bias_spec(lambda b, i: (jnp.minimum(i, 1), 0, 0)),
            bias_spec(lambda b, i: (1, 0, 0)),
            pl.BlockSpec((step_rows, dq), lambda b, i: (b * steps + i, 0)),
            pl.BlockSpec((WINDOW, dkv),
                         lambda b, i: ((b * steps + i) * SWA_BLOCKS_PER_STEP - jnp.minimum(i, 1), 0)),
            pl.BlockSpec((step_rows, dkv), lambda b, i: (b * steps + i, 0)),
        ],
        out_specs=pl.BlockSpec((step_rows, dq), lambda b, i: (b * steps + i, 0)),
        compiler_params=_params(("parallel", "arbitrary")),
        name="swa_attention",
    )(sinks, bias, bias, q, kv, kv)


def kernel(x, ffn1_norm, ffn1_w_in, ffn1_w_out, mix_norm, ffn2_norm, ffn2_w_in, ffn2_w_out,
           sb_w_qkv, sb_w_o, kv_norm, kv_w, swa_w_q, swa_sinks, swa_w_o, final_norm):
    batch, seq, d = x.shape
    n = batch * seq
    cos, sin = _rotary_tables(seq)
    bf = _to_bf16
    h = x.reshape(n, d)

    h, w_qkv = _ffn(h, ffn1_norm[0], bf(ffn1_w_in, 0), (ffn1_w_out, 0), side_cast=(sb_w_qkv, 0), name="ffn1_l0")
    qkv = _proj(h, mix_norm[0], w_qkv, cos, sin, rot_slabs=0, name="proj_qkv")
    later_weights = [(ffn2_w_in, 0), (ffn2_w_out, 0), (sb_w_o, 0), (kv_w, None), (ffn1_w_in, 1), (ffn1_w_out, 1),
                     (swa_w_q, 0), (ffn2_w_in, 1), (ffn2_w_out, 1), (swa_w_o, 0)]
    o, (w2_in0, w2_out0, w_o0, w_kv, w1_in1, w1_out1, w_q1, w2_in1, w2_out1, w_o1) = _sb_attention(
        qkv, batch, seq, later_weights)
    h = _ffn(h, ffn2_norm[0], w2_in0, w2_out0, attn=(o, w_o0), name="ffn2_l0")
    k_slabs = SWA_KV_HEADS * HEAD_DIM // LANES
    kv = _proj(h, kv_norm, w_kv, cos, sin, rot_slabs=k_slabs, name="proj_kv", dup_heads=True)

    h = _ffn(h, ffn1_norm[1], w1_in1, w1_out1, name="ffn1_l1")
    q = _proj(h, mix_norm[1], w_q1, cos, sin, rot_slabs=SWA_Q_HEADS * HEAD_DIM // LANES, name="proj_q")
    o = _swa_attention(q, kv, swa_sinks[0], batch, seq)
    h = _ffn(h, ffn2_norm[1], w2_in1, w2_out1, attn=(o, w_o1), final_g=final_norm, name="ffn2_l1")
    return h.reshape(batch, seq, d)
```
